```python
import jax, jax.numpy as jnp
from jax import lax
import numpy as np

D_MODEL = 1024
BATCH = 2
SEQ = 8192
DEPTH = 2

CHUNK = 64
N_MIXERS = 2
PLE_DIM = 256
EPS = 1e-6
POOL_WINDOWS = (2, 4, 8, 16)
POOL_GROUPS = len(POOL_WINDOWS)
POOL_GROUP_DIM = D_MODEL // POOL_GROUPS
MLA_HEADS = 8
Q_LORA = 384
KV_LORA = 256
NOPE_DIM = 128
ROPE_DIM = 64
V_DIM = 128
QK_DIM = NOPE_DIM + ROPE_DIM
ROPE_THETA = 10000.0
Q_BLOCK = 128
N_EXPERTS = 32
TOP_K = 4
D_FF = 1024
SWIGLU_LIMIT = 7.0
SWIGLU_ALPHA = 1.702
N_POOL_LAYERS = (DEPTH + 1) // 2
N_MLA_LAYERS = DEPTH // 2

kernel_name = "hybrid_pool_mla_moe_ple_trunk"


def rmsnorm(x, g):
    xf = x.astype(jnp.float32)
    out = xf * lax.rsqrt(jnp.mean(xf * xf, axis=-1, keepdims=True) + EPS) * g.astype(jnp.float32)
    return out.astype(x.dtype)


def pool_mixer(h, w, scale):
    B, S, D = h.shape
    hf = h.astype(jnp.float32)
    csum = jnp.concatenate([jnp.zeros((B, 1, D), jnp.float32), jnp.cumsum(hf, axis=1)], axis=1)
    pos1 = jnp.arange(1, S + 1)
    outs = []
    for g, win in enumerate(POOL_WINDOWS):
        sl = slice(g * POOL_GROUP_DIM, (g + 1) * POOL_GROUP_DIM)
        cg = csum[:, :, sl]
        lagged = jnp.concatenate([jnp.zeros((B, win, POOL_GROUP_DIM), jnp.float32), cg[:, :S + 1 - win]], axis=1)
        window_sum = cg[:, 1:] - lagged[:, 1:]
        count = jnp.minimum(pos1, win).astype(jnp.float32)[None, :, None]
        diff = (window_sum / count - hf[:, :, sl]).astype(h.dtype)
        outs.append(jnp.einsum('bsc,cd->bsd', diff, w[g]))
    return jnp.concatenate(outs, axis=-1) * scale


def rope_tables(positions):
    half = ROPE_DIM // 2
    inv_freq = ROPE_THETA ** (-jnp.arange(half, dtype=jnp.float32) / half)
    ang = positions.astype(jnp.float32)[..., None] * inv_freq
    return jnp.cos(ang)[:, :, None, :], jnp.sin(ang)[:, :, None, :]


def apply_rope(x, cos, sin):
    half = ROPE_DIM // 2
    xf = x.astype(jnp.float32)
    x1, x2 = xf[..., :half], xf[..., half:]
    return jnp.concatenate([x1 * cos - x2 * sin, x2 * cos + x1 * sin], axis=-1).astype(x.dtype)


def chunk_causal_attention(q, k, v):
    B, S, H, _ = q.shape
    n_blk = S // Q_BLOCK
    scale = QK_DIM ** -0.5
    k_chunk = jnp.arange(S) // CHUNK
    q_blocks = q.reshape(B, n_blk, Q_BLOCK, H, QK_DIM).transpose(1, 0, 2, 3, 4)

    def one_block(args):
        qb, blk = args
        s = jnp.einsum('bqhd,bkhd->bhqk', qb, k, preferred_element_type=jnp.float32) * scale
        q_chunk = (blk * Q_BLOCK + jnp.arange(Q_BLOCK)) // CHUNK
        mask = k_chunk[None, :] <= q_chunk[:, None]
        s = jnp.where(mask[None, None], s, -jnp.inf)
        pr = jax.nn.softmax(s, axis=-1).astype(v.dtype)
        return jnp.einsum('bhqk,bkhd->bqhd', pr, v)

    o = lax.map(one_block, (q_blocks, jnp.arange(n_blk)))
    return o.transpose(1, 0, 2, 3, 4).reshape(B, S, H, V_DIM)


def mla_mixer(h, positions, w_in, q_norm, w_qb, kv_norm, w_kvb, qh_norm, kh_norm, w_o):
    B, S, _ = h.shape
    lat = h @ w_in
    q_lat = lat[..., :Q_LORA]
    kv_lat = lat[..., Q_LORA:Q_LORA + KV_LORA]
    k_rope = lat[..., Q_LORA + KV_LORA:]
    q = (rmsnorm(q_lat, q_norm) @ w_qb).reshape(B, S, MLA_HEADS, QK_DIM)
    kv = (rmsnorm(kv_lat, kv_norm) @ w_kvb).reshape(B, S, MLA_HEADS, NOPE_DIM + V_DIM)
    k_nope, v = kv[..., :NOPE_DIM], kv[..., NOPE_DIM:]
    k = jnp.concatenate([k_nope, jnp.broadcast_to(k_rope[:, :, None, :], (B, S, MLA_HEADS, ROPE_DIM))], axis=-1)
    q = rmsnorm(q, qh_norm)
    k = rmsnorm(k, kh_norm)
    cos, sin = rope_tables(positions)
    q = jnp.concatenate([q[..., :NOPE_DIM], apply_rope(q[..., NOPE_DIM:], cos, sin)], axis=-1)
    k = jnp.concatenate([k[..., :NOPE_DIM], apply_rope(k[..., NOPE_DIM:], cos, sin)], axis=-1)
    o = chunk_causal_attention(q, k, v)
    return o.reshape(B, S, MLA_HEADS * V_DIM) @ w_o


def moe(h, router_w, router_b, w_gu, b_gu, w_down, b_down):
    B, S, D = h.shape
    t = h.reshape(-1, D)
    logits = (t @ router_w).astype(jnp.float32) + router_b.astype(jnp.float32)
    top_v, top_i = lax.top_k(logits, TOP_K)
    gates = jax.nn.softmax(top_v, axis=-1)
    flat_e = top_i.reshape(-1)
    order = jnp.argsort(flat_e)
    e_sorted = flat_e[order]
    tok = order // TOP_K
    xs = t[tok]
    sizes = jnp.bincount(flat_e, length=N_EXPERTS).astype(jnp.int32)
    gu = lax.ragged_dot(xs, w_gu, sizes) + b_gu[e_sorted]
    gate = jnp.minimum(gu[:, :D_FF], SWIGLU_LIMIT)
    up = jnp.clip(gu[:, D_FF:], -SWIGLU_LIMIT, SWIGLU_LIMIT)
    act = (up + 1) * gate * jax.nn.sigmoid(SWIGLU_ALPHA * gate)
    out = lax.ragged_dot(act, w_down, sizes) + b_down[e_sorted]
    out = out * gates.reshape(-1)[order][:, None].astype(out.dtype)
    y = jnp.zeros_like(t).at[tok].add(out)
    return y.reshape(B, S, D)


def setup_inputs(seed: int = 0) -> dict:
    key = jax.random.key(seed)
    ks = iter(jax.random.split(key, 32))
    nrm = lambda shape, s: jax.random.normal(next(ks), shape, jnp.float32) * s
    gain = lambda shape: 1.0 + 0.05 * jax.random.normal(next(ks), shape, jnp.float32)
    NA, NB = N_POOL_LAYERS, N_MLA_LAYERS
    offset = jax.random.randint(next(ks), (BATCH, 1), 0, 4096, dtype=jnp.int32)
    positions = (offset + jnp.arange(SEQ, dtype=jnp.int32)[None, :]).astype(jnp.int32)
    return {
        "x": nrm((BATCH, SEQ, D_MODEL), 1.0),
        "p": nrm((DEPTH, BATCH, SEQ, PLE_DIM), 1.0),
        "positions": positions,
        "norm_mix": gain((DEPTH, D_MODEL)),
        "pool_w": nrm((NA, POOL_GROUPS, POOL_GROUP_DIM, POOL_GROUP_DIM), POOL_GROUP_DIM ** -0.5),
        "pool_scale": gain((NA, D_MODEL)),
        "mla_w_in": nrm((NB, D_MODEL, Q_LORA + KV_LORA + ROPE_DIM), D_MODEL ** -0.5),
        "mla_q_norm": gain((NB, Q_LORA)),
        "mla_w_qb": nrm((NB, Q_LORA, MLA_HEADS * QK_DIM), Q_LORA ** -0.5),
        "mla_kv_norm": gain((NB, KV_LORA)),
        "mla_w_kvb": nrm((NB, KV_LORA, MLA_HEADS * (NOPE_DIM + V_DIM)), KV_LORA ** -0.5),
        "mla_qh_norm": gain((NB, QK_DIM)),
        "mla_kh_norm": gain((NB, QK_DIM)),
        "mla_w_o": nrm((NB, MLA_HEADS * V_DIM, D_MODEL), (MLA_HEADS * V_DIM) ** -0.5),
        "norm_ffn": gain((DEPTH, D_MODEL)),
        "router_w": nrm((DEPTH, D_MODEL, N_EXPERTS), D_MODEL ** -0.5),
        "router_b": nrm((DEPTH, N_EXPERTS), 0.01),
        "w_gate_up": nrm((DEPTH, N_EXPERTS, D_MODEL, 2 * D_FF), D_MODEL ** -0.5),
        "b_gate_up": nrm((DEPTH, N_EXPERTS, 2 * D_FF), 0.02),
        "w_down": nrm((DEPTH, N_EXPERTS, D_FF, D_MODEL), D_FF ** -0.5),
        "b_down": nrm((DEPTH, N_EXPERTS, D_MODEL), 0.02),
        "ple_norm": gain((DEPTH, D_MODEL)),
        "ple_gate_w": nrm((DEPTH, D_MODEL, D_MODEL), D_MODEL ** -0.5),
        "ple_proj_w": nrm((DEPTH, PLE_DIM, D_MODEL), PLE_DIM ** -0.5),
    }


def reference(x, p, positions, norm_mix, pool_w, pool_scale, mla_w_in, mla_q_norm, mla_w_qb,
              mla_kv_norm, mla_w_kvb, mla_qh_norm, mla_kh_norm, mla_w_o, norm_ffn, router_w,
              router_b, w_gate_up, b_gate_up, w_down, b_down, ple_norm, ple_gate_w, ple_proj_w):
    for i in range(DEPTH):
        h = rmsnorm(x, norm_mix[i])
        j = i // N_MIXERS
        if i % N_MIXERS == 0:
            x = x + pool_mixer(h, pool_w[j], pool_scale[j])
        else:
            x = x + mla_mixer(h, positions, mla_w_in[j], mla_q_norm[j], mla_w_qb[j], mla_kv_norm[j],
                              mla_w_kvb[j], mla_qh_norm[j], mla_kh_norm[j], mla_w_o[j])
        x = x + moe(rmsnorm(x, norm_ffn[i]), router_w[i], router_b[i], w_gate_up[i], b_gate_up[i],
                    w_down[i], b_down[i])
        ple_gate = jax.nn.sigmoid(rmsnorm(x, ple_norm[i]) @ ple_gate_w[i])
        x = x + ple_gate * (p[i] @ ple_proj_w[i])
    return x
```

```python
import functools

import jax
import jax.numpy as jnp
from jax import lax
from jax.experimental import pallas as pl
from jax.experimental.pallas import tpu as pltpu

F32 = jnp.float32
BF16 = jnp.bfloat16

EPS = 1e-6
POOL_WINDOWS = (2, 4, 8, 16)
POOL_HALO = 16
CHUNK = 64
MLA_HEADS = 8
Q_LORA = 384
KV_LORA = 256
NOPE_DIM = 128
ROPE_DIM = 64
V_DIM = 128
QK_DIM = NOPE_DIM + ROPE_DIM
HEAD_PAD = 256
ROPE_THETA = 10000.0
N_EXPERTS = 32
TOP_K = 4
SWIGLU_LIMIT = 7.0
SWIGLU_ALPHA = 1.702

V7X_VMEM_LIMIT = 56 * 1024 * 1024

POOL_TS = 512
ROUTER_TM = 512
DISPATCH_TM = 512
EXPERT_TM = 512
EXPERT_FC = 512
COMBINE_TM = 256
MLA_TM = 256
ATTN_TQ = 512
OPROJ_TM = 512
TRIG_ROWS = 512


def _params(*sem):
    return pltpu.CompilerParams(dimension_semantics=sem, vmem_limit_bytes=V7X_VMEM_LIMIT)


def _rms(x, g):
    return x * lax.rsqrt(jnp.mean(x * x, axis=-1, keepdims=True) + EPS) * g


def _dot(a, b):
    return jnp.dot(a, b, preferred_element_type=F32)


def _pool_kernel(x_ref, g_ref, w_ref, sc_ref, o_ref, hbuf):
    s = pl.program_id(1)
    ts = x_ref.shape[1]
    gd = w_ref.shape[1]
    x = x_ref[0]
    h = _rms(x, g_ref[...])

    @pl.when(s == 0)
    def _():
        hbuf[0:POOL_HALO, :] = jnp.zeros((POOL_HALO, x.shape[1]), F32)

    hbuf[POOL_HALO:, :] = h
    pos1 = s * ts + lax.broadcasted_iota(jnp.int32, (ts, 1), 0) + 1
    outs = []
    for g, win in enumerate(POOL_WINDOWS):
        lo, hi = g * gd, (g + 1) * gd
        hg = h[:, lo:hi]
        acc = hg
        for j in range(1, win):
            acc = acc + hbuf[POOL_HALO - j:POOL_HALO - j + ts, lo:hi]
        cnt = jnp.minimum(pos1, win).astype(F32)
        diff = acc / cnt - hg
        outs.append(_dot(diff.astype(BF16), w_ref[g]))
    mix = jnp.concatenate(outs, axis=-1) * sc_ref[...]
    o_ref[0] = x + mix
    hbuf[0:POOL_HALO, :] = hbuf[ts:ts + POOL_HALO, :]


def _pool_layer(x, g, w, scale):
    B, S, D = x.shape
    ts = POOL_TS
    G, gd, _ = w.shape
    return pl.pallas_call(
        _pool_kernel,
        out_shape=jax.ShapeDtypeStruct((B, S, D), F32),
        grid=(B, S // ts),
        in_specs=[
            pl.BlockSpec((1, ts, D), lambda b, s: (b, s, 0)),
            pl.BlockSpec((1, D), lambda b, s: (0, 0)),
            pl.BlockSpec((G, gd, gd), lambda b, s: (0, 0, 0)),
            pl.BlockSpec((1, D), lambda b, s: (0, 0)),
        ],
        out_specs=pl.BlockSpec((1, ts, D), lambda b, s: (b, s, 0)),
        scratch_shapes=[pltpu.VMEM((ts + POOL_HALO, D), F32)],
        compiler_params=_params("arbitrary", "arbitrary"),
        name="pool_mixer",
    )(x, g.reshape(1, D), w.astype(BF16), scale.reshape(1, D))


def _router_kernel(x_ref, g_ref, rwt_ref, rb_ref, xn_ref, ti_ref, gt_ref, rk_ref, cnt_ref, run):
    i = pl.program_id(0)
    tm = x_ref.shape[0]
    E = rwt_ref.shape[0]

    @pl.when(i == 0)
    def _():
        run[...] = jnp.zeros(run.shape, F32)

    xn = _rms(x_ref[...], g_ref[...])
    xn_ref[...] = xn
    logits = lax.dot_general(rwt_ref[...], xn, (((1,), (1,)), ((), ())),
                             precision=lax.Precision.HIGHEST,
                             preferred_element_type=F32) + rb_ref[...]
    eidx = lax.broadcasted_iota(jnp.int32, (E, tm), 0)
    vals, sels = [], []
    l = logits
    for k in range(TOP_K):
        m = jnp.max(l, axis=0, keepdims=True)
        idx = jnp.min(jnp.where(l == m, eidx, E), axis=0, keepdims=True)
        sel = eidx == idx
        ti_ref[k:k + 1, :] = idx
        vals.append(m)
        sels.append(sel)
        l = jnp.where(sel, -jnp.inf, l)
    ex = [jnp.exp(v - vals[0]) for v in vals]
    den = ex[0] + ex[1] + ex[2] + ex[3]
    for k in range(TOP_K):
        gt_ref[k:k + 1, :] = ex[k] / den
    member = jnp.zeros((E, tm), F32)
    for sel in sels:
        member = member + sel.astype(F32)
    r = lax.broadcasted_iota(jnp.int32, (tm, tm), 0)
    c = lax.broadcasted_iota(jnp.int32, (tm, tm), 1)
    before = jnp.where(r < c, 1.0, 0.0).astype(BF16)
    prior = _dot(member.astype(BF16), before) + run[:, 0:1]
    for k in range(TOP_K):
        rk = jnp.sum(jnp.where(sels[k], prior, 0.0), axis=0, keepdims=True)
        rk_ref[k:k + 1, :] = rk.astype(jnp.int32)
    run[...] = run[...] + jnp.sum(member, axis=1, keepdims=True)
    cnt_ref[...] = run[...]


def _router(x, g, rw, rb):
    T, D = x.shape
    E = rw.shape[1]
    tm = ROUTER_TM
    row = lambda i: (i, 0)
    col = lambda i: (0, i)
    const = lambda i: (0, 0)
    return pl.pallas_call(
        _router_kernel,
        out_shape=(
            jax.ShapeDtypeStruct((T, D), F32),
            jax.ShapeDtypeStruct((TOP_K, T), jnp.int32),
            jax.ShapeDtypeStruct((TOP_K, T), F32),
            jax.ShapeDtypeStruct((TOP_K, T), jnp.int32),
            jax.ShapeDtypeStruct((E, 128), F32),
        ),
        grid=(T // tm,),
        in_specs=[
            pl.BlockSpec((tm, D), row),
            pl.BlockSpec((1, D), const),
            pl.BlockSpec((E, D), const),
            pl.BlockSpec((E, 1), const),
        ],
        out_specs=(
            pl.BlockSpec((tm, D), row),
            pl.BlockSpec((TOP_K, tm), col),
            pl.BlockSpec((TOP_K, tm), col),
            pl.BlockSpec((TOP_K, tm), col),
            pl.BlockSpec((E, 128), const),
        ),
        scratch_shapes=[pltpu.VMEM((E, 128), F32)],
        compiler_params=_params("arbitrary"),
        name="moe_router",
    )(x, g.reshape(1, D), rw.T, rb.reshape(E, 1))


def _routing_tables(top_i, rank, counts, tm, n_tiles):
    counts = counts.astype(jnp.int32)
    padded = ((counts + tm - 1) // tm) * tm
    ends = jnp.cumsum(padded)
    offs = ends - padded
    pos = jnp.take(offs, top_i) + rank
    n_used = ends[-1] // tm
    tile_start = jnp.arange(n_tiles, dtype=jnp.int32) * tm
    te = jnp.sum((ends[None, :] <= tile_start[:, None]).astype(jnp.int32), axis=1)
    te = jnp.minimum(te, N_EXPERTS - 1)
    te = jnp.where(jnp.arange(n_tiles) < n_used, te, jnp.take(te, n_used - 1))
    return pos, te.astype(jnp.int32), n_used.reshape(1).astype(jnp.int32)


def _row_copy(src_ref, src_row, dst_ref, dst_row, sem):
    return pltpu.make_async_copy(src_ref.at[pl.ds(src_row, 1)], dst_ref.at[pl.ds(dst_row, 1)], sem)


def _dispatch_kernel(pos_ref, xn_ref, xs_in_ref, xs_ref, sem):
    del xs_in_ref
    tm = xn_ref.shape[0]

    def issue(j, carry):
        for k in range(TOP_K):
            _row_copy(xn_ref, j, xs_ref, pos_ref[k, j], sem).start()
        return carry

    lax.fori_loop(0, tm, issue, 0, unroll=8)

    def drain(j, carry):
        for k in range(TOP_K):
            _row_copy(xn_ref, j, xs_ref, pos_ref[k, j], sem).wait()
        return carry

    lax.fori_loop(0, tm, drain, 0, unroll=8)


def _dispatch(xn, pos, n_rows):
    T, D = xn.shape
    tm = DISPATCH_TM
    zeros = jnp.zeros((n_rows, D), F32)
    return pl.pallas_call(
        _dispatch_kernel,
        out_shape=jax.ShapeDtypeStruct((n_rows, D), F32),
        grid=(T // tm,),
        in_specs=[
            pl.BlockSpec((TOP_K, tm), lambda i: (0, i), memory_space=pltpu.SMEM),
            pl.BlockSpec((tm, D), lambda i: (i, 0)),
            pl.BlockSpec(memory_space=pl.ANY),
        ],
        out_specs=pl.BlockSpec(memory_space=pl.ANY),
        scratch_shapes=[pltpu.SemaphoreType.DMA],
        input_output_aliases={2: 0},
        compiler_params=_params("arbitrary"),
        name="moe_dispatch",
    )(pos, xn, zeros)


def _expert_kernel(te_ref, nu_ref, xs_ref, wgu_ref, bgu_ref, wd_ref, bd_ref, ys_ref,
                   wgu_bf, wd_bf, act):
    i = pl.program_id(0)
    F = wd_ref.shape[1]
    e = te_ref[i]
    prev = te_ref[jnp.maximum(i - 1, 0)]
    active = i < nu_ref[0]
    new_expert = jnp.logical_or(i == 0, e != prev)

    @pl.when(jnp.logical_and(active, new_expert))
    def _():
        wgu_bf[...] = wgu_ref[0].astype(BF16)
        wd_bf[...] = wd_ref[0].astype(BF16)

    @pl.when(active)
    def _():
        x = xs_ref[...].astype(BF16)
        fc = EXPERT_FC
        for c in range(F // fc):
            gate = _dot(x, wgu_bf[:, c * fc:(c + 1) * fc]) + bgu_ref[0, :, c * fc:(c + 1) * fc]
            up = _dot(x, wgu_bf[:, F + c * fc:F + (c + 1) * fc]) + bgu_ref[0, :, F + c * fc:F + (c + 1) * fc]
            gate = jnp.minimum(gate, SWIGLU_LIMIT)
            up = jnp.clip(up, -SWIGLU_LIMIT, SWIGLU_LIMIT)
            a = (up + 1.0) * gate * jax.nn.sigmoid(SWIGLU_ALPHA * gate)
            act[:, c * fc:(c + 1) * fc] = a.astype(BF16)
        ys_ref[...] = _dot(act[...], wd_bf[...]) + bd_ref[0]

    @pl.when(jnp.logical_not(active))
    def _():
        ys_ref[...] = jnp.zeros(ys_ref.shape, F32)


def _experts(xs, te, n_used, w_gu, b_gu, w_down, b_down):
    P, D = xs.shape
    E, _, F2 = w_gu.shape
    F = F2 // 2
    tm = EXPERT_TM
    n_tiles = P // tm
    tile = lambda i, te, nu: (jnp.minimum(i, nu[0] - 1), 0)
    wsel = lambda i, te, nu: (te[i], 0, 0)
    return pl.pallas_call(
        _expert_kernel,
        out_shape=jax.ShapeDtypeStruct((P, D), F32),
        grid_spec=pltpu.PrefetchScalarGridSpec(
            num_scalar_prefetch=2,
            grid=(n_tiles,),
            in_specs=[
                pl.BlockSpec((tm, D), tile),
                pl.BlockSpec((1, D, F2), wsel),
                pl.BlockSpec((1, 1, F2), wsel),
                pl.BlockSpec((1, F, D), wsel),
                pl.BlockSpec((1, 1, D), wsel),
            ],
            out_specs=pl.BlockSpec((tm, D), lambda i, te, nu: (i, 0)),
            scratch_shapes=[
                pltpu.VMEM((D, F2), BF16),
                pltpu.VMEM((F, D), BF16),
                pltpu.VMEM((tm, F), BF16),
            ],
        ),
        compiler_params=_params("arbitrary"),
        name="moe_experts",
    )(te, n_used, xs, w_gu, b_gu.reshape(E, 1, F2), w_down, b_down.reshape(E, 1, D))


def _combine_kernel(pos_ref, gt_ref, x_ref, ys_ref, p_ref, g_ref, gw_ref, pw_ref, o_ref, buf, sem):
    tm = x_ref.shape[0]

    def issue(j, carry):
        for k in range(TOP_K):
            _row_copy(ys_ref, pos_ref[k, j], buf.at[k], j, sem).start()
        return carry

    lax.fori_loop(0, tm, issue, 0, unroll=8)
    proj = _dot(p_ref[...].astype(BF16), pw_ref[...])

    def drain(j, carry):
        for k in range(TOP_K):
            _row_copy(ys_ref, pos_ref[k, j], buf.at[k], j, sem).wait()
        return carry

    lax.fori_loop(0, tm, drain, 0, unroll=8)
    gt = gt_ref[...]
    y = x_ref[...]
    for k in range(TOP_K):
        y = y + buf[k] * gt[:, k:k + 1]
    h = _rms(y, g_ref[...])
    gate = jax.nn.sigmoid(_dot(h.astype(BF16), gw_ref[...]))
    o_ref[...] = y + gate * proj


def _combine_ple(x, ys, pos, gates_t, p, g, gate_w, proj_w):
    T, D = x.shape
    PD = p.shape[1]
    tm = COMBINE_TM
    row = lambda i: (i, 0)
    const = lambda i: (0, 0)
    return pl.pallas_call(
        _combine_kernel,
        out_shape=jax.ShapeDtypeStruct((T, D), F32),
        grid=(T // tm,),
        in_specs=[
            pl.BlockSpec((TOP_K, tm), lambda i: (0, i), memory_space=pltpu.SMEM),
            pl.BlockSpec((tm, TOP_K), row),
            pl.BlockSpec((tm, D), row),
            pl.BlockSpec(memory_space=pl.ANY),
            pl.BlockSpec((tm, PD), row),
            pl.BlockSpec((1, D), const),
            pl.BlockSpec((D, D), const),
            pl.BlockSpec((PD, D), const),
        ],
        out_specs=pl.BlockSpec((tm, D), row),
        scratch_shapes=[pltpu.VMEM((TOP_K, tm, D), F32), pltpu.SemaphoreType.DMA],
        compiler_params=_params("arbitrary"),
        name="moe_combine_ple",
    )(pos, gates_t.T, x, ys, p, g.reshape(1, D), gate_w.astype(BF16), proj_w.astype(BF16))


def _moe_ple(x, p, norm_ffn, router_w, router_b, w_gu, b_gu, w_down, b_down, ple_norm, gate_w, proj_w):
    T, D = x.shape
    n_rows = T * TOP_K + N_EXPERTS * EXPERT_TM
    xn, top_i, gates_t, rank, cnt = _router(x, norm_ffn, router_w, router_b)
    pos, te, n_used = _routing_tables(top_i, rank, cnt[:, 0], EXPERT_TM, n_rows // EXPERT_TM)
    xs = _dispatch(xn, pos, n_rows)
    ys = _experts(xs, te, n_used, w_gu, b_gu, w_down, b_down)
    return _combine_ple(x, ys, pos, gates_t, p, ple_norm, gate_w, proj_w)


def _trig_kernel(pos_ref, f_ref, cos_ref, sin_ref):
    ang = pos_ref[...] * f_ref[...]
    cos_ref[...] = jnp.cos(ang)
    sin_ref[...] = jnp.sin(ang)


def _rope_tables(positions):
    T = positions.size
    half = ROPE_DIM // 2
    per_row = 128 // half
    inv_freq = ROPE_THETA ** (-jnp.arange(half, dtype=F32) / half)
    pos_rep = jnp.repeat(positions.reshape(T // per_row, per_row).astype(F32), half, axis=1)
    rows = T // per_row
    tr = TRIG_ROWS
    spec = pl.BlockSpec((tr, 128), lambda i: (i, 0))
    cos, sin = pl.pallas_call(
        _trig_kernel,
        out_shape=(jax.ShapeDtypeStruct((rows, 128), F32),) * 2,
        grid=(rows // tr,),
        in_specs=[spec, pl.BlockSpec((1, 128), lambda i: (0, 0))],
        out_specs=(spec, spec),
        compiler_params=_params("arbitrary"),
        name="rope_trig",
    )(pos_rep, jnp.tile(inv_freq, per_row).reshape(1, 128))
    cos = cos.reshape(T, half)
    sin = sin.reshape(T, half)
    z = jnp.zeros_like(cos)
    c_tab = jnp.concatenate([cos, cos, z, z], axis=1)
    s_lo = jnp.concatenate([-sin, z, z, z], axis=1)
    s_hi = jnp.concatenate([z, sin, z, z], axis=1)
    return c_tab, s_lo, s_hi


def _rope(blk, c_tab, s_lo, s_hi):
    half = ROPE_DIM // 2
    return (blk * c_tab + pltpu.roll(blk, 128 - half, 1) * s_lo + pltpu.roll(blk, half, 1) * s_hi)


def _mla_pre_kernel(x_ref, g_ref, win_ref, qn_ref, wq_ref, kvn_ref, wkv_ref, gq_ref, gk_ref,
                    c_ref, sl_ref, sh_ref, q_ref, k_ref, v_ref):
    H = q_ref.shape[1]
    h = _rms(x_ref[...], g_ref[...])
    lat = _dot(h.astype(BF16), win_ref[...])
    ql = _rms(lat[:, :Q_LORA], qn_ref[...])
    kvl = _rms(lat[:, Q_LORA:Q_LORA + KV_LORA], kvn_ref[...])
    kr = lat[:, Q_LORA + KV_LORA:]
    q = _dot(ql.astype(BF16), wq_ref[...])
    kv = _dot(kvl.astype(BF16), wkv_ref[...])
    c_tab, s_lo, s_hi = c_ref[...], sl_ref[...], sh_ref[...]
    gq, gk = gq_ref[...], gk_ref[...]
    kr_ss = jnp.sum(kr * kr, axis=-1, keepdims=True)
    kr_rot = _rope(kr * gk[:, NOPE_DIM:], c_tab, s_lo, s_hi)
    qscale = QK_DIM ** -0.5
    for hd in range(H):
        q0 = q[:, hd * HEAD_PAD:hd * HEAD_PAD + NOPE_DIM]
        q1 = q[:, hd * HEAD_PAD + NOPE_DIM:(hd + 1) * HEAD_PAD]
        ss = jnp.sum(q0 * q0 + q1 * q1, axis=-1, keepdims=True)
        inv = lax.rsqrt(ss / QK_DIM + EPS) * qscale
        q_ref[0, hd, :, :NOPE_DIM] = (q0 * inv * gq[:, :NOPE_DIM]).astype(BF16)
        q_ref[0, hd, :, NOPE_DIM:] = (_rope(q1 * gq[:, NOPE_DIM:], c_tab, s_lo, s_hi) * inv).astype(BF16)
        k0 = kv[:, hd * 2 * NOPE_DIM:hd * 2 * NOPE_DIM + NOPE_DIM]
        ssk = jnp.sum(k0 * k0, axis=-1, keepdims=True) + kr_ss
        invk = lax.rsqrt(ssk / QK_DIM + EPS)
        k_ref[0, hd, :, :NOPE_DIM] = (k0 * invk * gk[:, :NOPE_DIM]).astype(BF16)
        k_ref[0, hd, :, NOPE_DIM:] = (kr_rot * invk).astype(BF16)
        v_ref[0, hd] = kv[:, hd * 2 * NOPE_DIM + NOPE_DIM:(hd + 1) * 2 * NOPE_DIM].astype(BF16)


def _mla_pre(x, positions, g, w_in, q_norm, w_qb, kv_norm, w_kvb, qh_norm, kh_norm):
    B, S, D = x.shape
    H = MLA_HEADS
    tm = MLA_TM
    n_lat = Q_LORA + KV_LORA
    w_in_p = jnp.concatenate([w_in, jnp.zeros((D, 128 - ROPE_DIM), F32)], axis=1).astype(BF16)
    wq = w_qb.reshape(Q_LORA, H, QK_DIM)
    wq_p = jnp.concatenate([wq, jnp.zeros((Q_LORA, H, HEAD_PAD - QK_DIM), F32)], axis=2)
    wq_p = wq_p.reshape(Q_LORA, H * HEAD_PAD).astype(BF16)
    pad_gain = lambda n: jnp.concatenate([n, jnp.zeros((HEAD_PAD - QK_DIM,), F32)]).reshape(1, HEAD_PAD)
    c_tab, s_lo, s_hi = _rope_tables(positions)
    row = lambda b, s: (b * (S // tm) + s, 0)
    const = lambda b, s: (0, 0)
    head_spec = lambda w: pl.BlockSpec((1, H, tm, w), lambda b, s: (b, 0, s, 0))
    tab_spec = pl.BlockSpec((tm, 128), row)
    return pl.pallas_call(
        _mla_pre_kernel,
        out_shape=(
            jax.ShapeDtypeStruct((B, H, S, HEAD_PAD), BF16),
            jax.ShapeDtypeStruct((B, H, S, HEAD_PAD), BF16),
            jax.ShapeDtypeStruct((B, H, S, V_DIM), BF16),
        ),
        grid=(B, S // tm),
        in_specs=[
            pl.BlockSpec((tm, D), row),
            pl.BlockSpec((1, D), const),
            pl.BlockSpec((D, n_lat + 128), const),
            pl.BlockSpec((1, Q_LORA), const),
            pl.BlockSpec((Q_LORA, H * HEAD_PAD), const),
            pl.BlockSpec((1, KV_LORA), const),
            pl.BlockSpec((KV_LORA, H * 2 * NOPE_DIM), const),
            pl.BlockSpec((1, HEAD_PAD), const),
            pl.BlockSpec((1, HEAD_PAD), const),
            tab_spec, tab_spec, tab_spec,
        ],
        out_specs=(head_spec(HEAD_PAD), head_spec(HEAD_PAD), head_spec(V_DIM)),
        compiler_params=_params("arbitrary", "arbitrary"),
        name="mla_qkv",
    )(x.reshape(B * S, D), g.reshape(1, D), w_in_p, q_norm.reshape(1, Q_LORA), wq_p,
      kv_norm.reshape(1, KV_LORA), w_kvb.astype(BF16), pad_gain(qh_norm), pad_gain(kh_norm),
      c_tab, s_lo, s_hi)


def _attn_kernel(q_ref, k_ref, v_ref, o_ref, m_scr, l_scr, acc_scr):
    qi = pl.program_id(2)
    tq = q_ref.shape[2]
    q = q_ref[0, 0]
    m_scr[...] = jnp.full(m_scr.shape, -jnp.inf, F32)
    l_scr[...] = jnp.zeros(l_scr.shape, F32)
    acc_scr[...] = jnp.zeros(acc_scr.shape, F32)

    def step(ki, diagonal):
        start = pl.multiple_of(ki * tq, tq)
        k = k_ref[0, 0, pl.ds(start, tq), :]
        v = v_ref[0, 0, pl.ds(start, tq), :]
        s = lax.dot_general(q, k, (((1,), (1,)), ((), ())), preferred_element_type=F32)
        if diagonal:
            r = lax.broadcasted_iota(jnp.int32, (tq, tq), 0) // CHUNK
            c = lax.broadcasted_iota(jnp.int32, (tq, tq), 1) // CHUNK
            s = jnp.where(c <= r, s, -jnp.inf)
        m_prev = m_scr[...]
        m_new = jnp.maximum(m_prev, jnp.max(s, axis=-1, keepdims=True))
        alpha = jnp.exp(m_prev - m_new)
        p = jnp.exp(s - m_new)
        l_scr[...] = alpha * l_scr[...] + jnp.sum(p, axis=-1, keepdims=True)
        acc_scr[...] = alpha * acc_scr[...] + _dot(p.astype(BF16), v)
        m_scr[...] = m_new

    def body(ki, carry):
        step(ki, False)
        return carry

    lax.fori_loop(0, qi, body, 0)
    step(qi, True)
    o_ref[0] = (acc_scr[...] / l_scr[...]).astype(o_ref.dtype)


def _attention(q, k, v):
    B, H, S, _ = q.shape
    tq = ATTN_TQ
    return pl.pallas_call(
        _attn_kernel,
        out_shape=jax.ShapeDtypeStruct((B, S, H * V_DIM), BF16),
        grid=(B, H, S // tq),
        in_specs=[
            pl.BlockSpec((1, 1, tq, HEAD_PAD), lambda b, h, i: (b, h, i, 0)),
            pl.BlockSpec((1, 1, S, HEAD_PAD), lambda b, h, i: (b, h, 0, 0)),
            pl.BlockSpec((1, 1, S, V_DIM), lambda b, h, i: (b, h, 0, 0)),
        ],
        out_specs=pl.BlockSpec((1, tq, V_DIM), lambda b, h, i: (b, i, h)),
        scratch_shapes=[
            pltpu.VMEM((tq, 1), F32),
            pltpu.VMEM((tq, 1), F32),
            pltpu.VMEM((tq, V_DIM), F32),
        ],
        compiler_params=_params("arbitrary", "arbitrary", "arbitrary"),
        name="mla_attention",
    )(q, k, v)


def _oproj_kernel(x_ref, o_ref, w_ref, out_ref):
    out_ref[...] = x_ref[...] + _dot(o_ref[...], w_ref[...])


def _oproj(x, o, w_o):
    T, D = x.shape
    tm = OPROJ_TM
    row = lambda i: (i, 0)
    return pl.pallas_call(
        _oproj_kernel,
        out_shape=jax.ShapeDtypeStruct((T, D), F32),
        grid=(T // tm,),
        in_specs=[
            pl.BlockSpec((tm, D), row),
            pl.BlockSpec((tm, o.shape[1]), row),
            pl.BlockSpec(w_o.shape, lambda i: (0, 0)),
        ],
        out_specs=pl.BlockSpec((tm, D), row),
        compiler_params=_params("arbitrary"),
        name="mla_oproj",
    )(x, o, w_o.astype(BF16))


def _mla_layer(x, positions, g, w_in, q_norm, w_qb, kv_norm, w_kvb, qh_norm, kh_norm, w_o):
    B, S, D = x.shape
    q, k, v = _mla_pre(x, positions, g, w_in, q_norm, w_qb, kv_norm, w_kvb, qh_norm, kh_norm)
    o = _attention(q, k, v)
    return _oproj(x.reshape(B * S, D), o.reshape(B * S, -1), w_o).reshape(B, S, D)


def kernel(x, p, positions, norm_mix, pool_w, pool_scale, mla_w_in, mla_q_norm, mla_w_qb, mla_kv_norm, mla_w_kvb, mla_qh_norm, mla_kh_norm, mla_w_o, norm_ffn, router_w, router_b, w_gate_up, b_gate_up, w_down, b_down, ple_norm, ple_gate_w, ple_proj_w):
    B, S, D = x.shape
    depth = norm_mix.shape[0]
    for i in range(depth):
        j = i // 2
        if i % 2 == 0:
            x = _pool_layer(x, norm_mix[i], pool_w[j], pool_scale[j])
        else:
            x = _mla_layer(x, positions, norm_mix[i], mla_w_in[j], mla_q_norm[j], mla_w_qb[j],
                           mla_kv_norm[j], mla_w_kvb[j], mla_qh_norm[j], mla_kh_norm[j], mla_w_o[j])
        x = _moe_ple(x.reshape(B * S, D), p[i].reshape(B * S, -1), norm_ffn[i], router_w[i],
                     router_b[i], w_gate_up[i], b_gate_up[i], w_down[i], b_down[i], ple_norm[i],
                     ple_gate_w[i], ple_proj_w[i]).reshape(B, S, D)
    return x
```

```python
import functools

import jax
import jax.numpy as jnp
from jax import lax
from jax.experimental import pallas as pl
from jax.experimental.pallas import tpu as pltpu

F32 = jnp.float32
BF16 = jnp.bfloat16

EPS = 1e-6
POOL_WINDOWS = (2, 4, 8, 16)
POOL_HALO = 16
CHUNK = 64
MLA_HEADS = 8
Q_LORA = 384
KV_LORA = 256
NOPE_DIM = 128
ROPE_DIM = 64
V_DIM = 128
QK_DIM = NOPE_DIM + ROPE_DIM
HEAD_PAD = 256
ROPE_THETA = 10000.0
LOG2_E = 1.4426950408889634
N_EXPERTS = 32
TOP_K = 4
SWIGLU_LIMIT = 7.0
SWIGLU_ALPHA = 1.702

V7X_VMEM_LIMIT = 56 * 1024 * 1024

POOL_TS = 512
ROUTER_TM = 512
DISPATCH_TM = 512
EXPERT_TM = 512
EXPERT_FC = 512
COMBINE_TM = 256
MLA_TM = 256
ATTN_TQ = 512
OPROJ_TM = 512
TRIG_ROWS = 512


def _params(*sem):
    return pltpu.CompilerParams(dimension_semantics=sem, vmem_limit_bytes=V7X_VMEM_LIMIT)


def _rms(x, g):
    return x * lax.rsqrt(jnp.mean(x * x, axis=-1, keepdims=True) + EPS) * g


def _dot(a, b):
    return jnp.dot(a, b, preferred_element_type=F32)


def _pool_kernel(x_ref, g_ref, w_ref, sc_ref, o_ref, hbuf):
    s = pl.program_id(1)
    ts = x_ref.shape[1]
    gd = w_ref.shape[1]
    x = x_ref[0]
    h = _rms(x, g_ref[...])

    @pl.when(s == 0)
    def _():
        hbuf[0:POOL_HALO, :] = jnp.zeros((POOL_HALO, x.shape[1]), F32)

    hbuf[POOL_HALO:, :] = h
    pos1 = s * ts + lax.broadcasted_iota(jnp.int32, (ts, 1), 0) + 1
    outs = []
    for g, win in enumerate(POOL_WINDOWS):
        lo, hi = g * gd, (g + 1) * gd
        hg = h[:, lo:hi]
        acc = hg
        for j in range(1, win):
            acc = acc + hbuf[POOL_HALO - j:POOL_HALO - j + ts, lo:hi]
        cnt = jnp.minimum(pos1, win).astype(F32)
        diff = acc / cnt - hg
        outs.append(_dot(diff.astype(BF16), w_ref[g]))
    mix = jnp.concatenate(outs, axis=-1) * sc_ref[...]
    o_ref[0] = x + mix
    hbuf[0:POOL_HALO, :] = hbuf[ts:ts + POOL_HALO, :]


def _pool_layer(x, g, w, scale):
    B, S, D = x.shape
    ts = POOL_TS
    G, gd, _ = w.shape
    return pl.pallas_call(
        _pool_kernel,
        out_shape=jax.ShapeDtypeStruct((B, S, D), F32),
        grid=(B, S // ts),
        in_specs=[
            pl.BlockSpec((1, ts, D), lambda b, s: (b, s, 0)),
            pl.BlockSpec((1, D), lambda b, s: (0, 0)),
            pl.BlockSpec((G, gd, gd), lambda b, s: (0, 0, 0)),
            pl.BlockSpec((1, D), lambda b, s: (0, 0)),
        ],
        out_specs=pl.BlockSpec((1, ts, D), lambda b, s: (b, s, 0)),
        scratch_shapes=[pltpu.VMEM((ts + POOL_HALO, D), F32)],
        compiler_params=_params("arbitrary", "arbitrary"),
        name="pool_mixer",
    )(x, g.reshape(1, D), w.astype(BF16), scale.reshape(1, D))


def _router_kernel(x_ref, g_ref, rwt_ref, rb_ref, xn_ref, ti_ref, gt_ref, rk_ref, cnt_ref, run):
    i = pl.program_id(0)
    tm = x_ref.shape[0]
    E = rwt_ref.shape[0]

    @pl.when(i == 0)
    def _():
        run[...] = jnp.zeros(run.shape, F32)

    xn = _rms(x_ref[...], g_ref[...])
    xn_ref[...] = xn
    logits = lax.dot_general(rwt_ref[...], xn, (((1,), (1,)), ((), ())),
                             precision=lax.Precision.HIGHEST,
                             preferred_element_type=F32) + rb_ref[...]
    eidx = lax.broadcasted_iota(jnp.int32, (E, tm), 0)
    vals, sels = [], []
    l = logits
    for k in range(TOP_K):
        m = jnp.max(l, axis=0, keepdims=True)
        idx = jnp.min(jnp.where(l == m, eidx, E), axis=0, keepdims=True)
        sel = eidx == idx
        ti_ref[k:k + 1, :] = idx
        vals.append(m)
        sels.append(sel)
        l = jnp.where(sel, -jnp.inf, l)
    ex = [jnp.exp(v - vals[0]) for v in vals]
    den = ex[0] + ex[1] + ex[2] + ex[3]
    for k in range(TOP_K):
        gt_ref[k:k + 1, :] = ex[k] / den
    member = jnp.zeros((E, tm), F32)
    for sel in sels:
        member = member + sel.astype(F32)
    r = lax.broadcasted_iota(jnp.int32, (tm, tm), 0)
    c = lax.broadcasted_iota(jnp.int32, (tm, tm), 1)
    before = jnp.where(r < c, 1.0, 0.0).astype(BF16)
    prior = _dot(member.astype(BF16), before) + run[:, 0:1]
    for k in range(TOP_K):
        rk = jnp.sum(jnp.where(sels[k], prior, 0.0), axis=0, keepdims=True)
        rk_ref[k:k + 1, :] = rk.astype(jnp.int32)
    run[...] = run[...] + jnp.sum(member, axis=1, keepdims=True)
    cnt_ref[...] = run[...]


def _router(x, g, rw, rb):
    T, D = x.shape
    E = rw.shape[1]
    tm = ROUTER_TM
    row = lambda i: (i, 0)
    col = lambda i: (0, i)
    const = lambda i: (0, 0)
    return pl.pallas_call(
        _router_kernel,
        out_shape=(
            jax.ShapeDtypeStruct((T, D), F32),
            jax.ShapeDtypeStruct((TOP_K, T), jnp.int32),
            jax.ShapeDtypeStruct((TOP_K, T), F32),
            jax.ShapeDtypeStruct((TOP_K, T), jnp.int32),
            jax.ShapeDtypeStruct((E, 128), F32),
        ),
        grid=(T // tm,),
        in_specs=[
            pl.BlockSpec((tm, D), row),
            pl.BlockSpec((1, D), const),
            pl.BlockSpec((E, D), const),
            pl.BlockSpec((E, 1), const),
        ],
        out_specs=(
            pl.BlockSpec((tm, D), row),
            pl.BlockSpec((TOP_K, tm), col),
            pl.BlockSpec((TOP_K, tm), col),
            pl.BlockSpec((TOP_K, tm), col),
            pl.BlockSpec((E, 128), const),
        ),
        scratch_shapes=[pltpu.VMEM((E, 128), F32)],
        compiler_params=_params("arbitrary"),
        name="moe_router",
    )(x, g.reshape(1, D), rw.T, rb.reshape(E, 1))


def _routing_tables(top_i, rank, counts, tm, n_tiles):
    counts = counts.astype(jnp.int32)
    padded = ((counts + tm - 1) // tm) * tm
    ends = jnp.cumsum(padded)
    offs = ends - padded
    e_ids = jnp.arange(N_EXPERTS, dtype=jnp.int32)[:, None, None]
    pos = rank + jnp.sum(jnp.where(top_i[None] == e_ids, offs[:, None, None], 0), axis=0)
    n_used = ends[-1] // tm
    tiles = jnp.arange(n_tiles, dtype=jnp.int32)
    te = jnp.sum((ends[None, :] <= (tiles * tm)[:, None]).astype(jnp.int32), axis=1)
    te = jnp.minimum(te, N_EXPERTS - 1)
    te = jnp.where(tiles < n_used, te, jnp.take(te, n_used - 1))
    cand = jnp.concatenate([ends // tm - 1, tiles])
    valid = jnp.concatenate([padded > 0, tiles >= n_used])
    order = jnp.argsort(jnp.logical_not(valid), stable=True)
    ztiles = jnp.concatenate([jnp.take(cand, order), jnp.sum(valid.astype(jnp.int32)).reshape(1)])
    return (pos, te.astype(jnp.int32), n_used.reshape(1).astype(jnp.int32), ztiles.astype(jnp.int32))


def _row_copy(src_ref, src_row, dst_ref, dst_row, sem):
    return pltpu.make_async_copy(src_ref.at[pl.ds(src_row, 1)], dst_ref.at[pl.ds(dst_row, 1)], sem)


def _dispatch_kernel(zt_ref, pos_ref, xn_ref, xs_ref, zbuf, sem):
    tm = xn_ref.shape[0]
    zt = zbuf.shape[0]

    @pl.when(pl.program_id(0) == 0)
    def _():
        zbuf[...] = jnp.zeros(zbuf.shape, F32)
        nz = zt_ref[zt_ref.shape[0] - 1]

        def zero_copy(t):
            row = pl.multiple_of(zt_ref[t] * zt, zt)
            return pltpu.make_async_copy(zbuf, xs_ref.at[pl.ds(row, zt)], sem)

        def zstart(t, carry):
            zero_copy(t).start()
            return carry

        def zwait(t, carry):
            zero_copy(t).wait()
            return carry

        lax.fori_loop(0, nz, zstart, 0)
        lax.fori_loop(0, nz, zwait, 0)

    def issue(j, carry):
        for k in range(TOP_K):
            _row_copy(xn_ref, j, xs_ref, pos_ref[k, j], sem).start()
        return carry

    lax.fori_loop(0, tm, issue, 0, unroll=8)

    def drain(j, carry):
        for k in range(TOP_K):
            _row_copy(xn_ref, j, xs_ref, pos_ref[k, j], sem).wait()
        return carry

    lax.fori_loop(0, tm, drain, 0, unroll=8)


def _dispatch(xn, pos, ztiles, n_rows):
    T, D = xn.shape
    tm = DISPATCH_TM
    return pl.pallas_call(
        _dispatch_kernel,
        out_shape=jax.ShapeDtypeStruct((n_rows, D), F32),
        grid_spec=pltpu.PrefetchScalarGridSpec(
            num_scalar_prefetch=1,
            grid=(T // tm,),
            in_specs=[
                pl.BlockSpec((TOP_K, tm), lambda i, zt: (0, i), memory_space=pltpu.SMEM),
                pl.BlockSpec((tm, D), lambda i, zt: (i, 0)),
            ],
            out_specs=pl.BlockSpec(memory_space=pl.ANY),
            scratch_shapes=[pltpu.VMEM((EXPERT_TM, D), F32), pltpu.SemaphoreType.DMA],
        ),
        compiler_params=_params("arbitrary"),
        name="moe_dispatch",
    )(ztiles, pos, xn)


def _expert_kernel(te_ref, nu_ref, xs_ref, wgu_ref, bgu_ref, wd_ref, bd_ref, ys_ref,
                   wgu_bf, wd_bf, act):
    i = pl.program_id(0)
    F = wd_ref.shape[1]
    e = te_ref[i]
    prev = te_ref[jnp.maximum(i - 1, 0)]
    active = i < nu_ref[0]
    new_expert = jnp.logical_or(i == 0, e != prev)

    @pl.when(jnp.logical_and(active, new_expert))
    def _():
        wgu_bf[...] = wgu_ref[0].astype(BF16)
        wd_bf[...] = wd_ref[0].astype(BF16)

    @pl.when(active)
    def _():
        x = xs_ref[...].astype(BF16)
        fc = EXPERT_FC
        for c in range(F // fc):
            gate = _dot(x, wgu_bf[:, c * fc:(c + 1) * fc]) + bgu_ref[0, :, c * fc:(c + 1) * fc]
            up = _dot(x, wgu_bf[:, F + c * fc:F + (c + 1) * fc]) + bgu_ref[0, :, F + c * fc:F + (c + 1) * fc]
            gate = jnp.minimum(gate, SWIGLU_LIMIT)
            up = jnp.clip(up, -SWIGLU_LIMIT, SWIGLU_LIMIT)
            a = (up + 1.0) * gate * jax.nn.sigmoid(SWIGLU_ALPHA * gate)
            act[:, c * fc:(c + 1) * fc] = a.astype(BF16)
        ys_ref[...] = _dot(act[...], wd_bf[...]) + bd_ref[0]

    @pl.when(jnp.logical_not(active))
    def _():
        ys_ref[...] = jnp.zeros(ys_ref.shape, F32)


def _experts(xs, te, n_used, layer, w_gu, b_gu, w_down, b_down):
    P, D = xs.shape
    L, E, _, F2 = w_gu.shape
    F = F2 // 2
    tm = EXPERT_TM
    n_tiles = P // tm
    tile = lambda i, te, nu: (jnp.minimum(i, nu[0] - 1), 0)
    wsel = lambda i, te, nu: (layer, te[i], 0, 0)
    return pl.pallas_call(
        _expert_kernel,
        out_shape=jax.ShapeDtypeStruct((P, D), F32),
        grid_spec=pltpu.PrefetchScalarGridSpec(
            num_scalar_prefetch=2,
            grid=(n_tiles,),
            in_specs=[
                pl.BlockSpec((tm, D), tile),
                pl.BlockSpec((None, 1, D, F2), wsel),
                pl.BlockSpec((None, 1, 1, F2), wsel),
                pl.BlockSpec((None, 1, F, D), wsel),
                pl.BlockSpec((None, 1, 1, D), wsel),
            ],
            out_specs=pl.BlockSpec((tm, D), lambda i, te, nu: (i, 0)),
            scratch_shapes=[
                pltpu.VMEM((D, F2), BF16),
                pltpu.VMEM((F, D), BF16),
                pltpu.VMEM((tm, F), BF16),
            ],
        ),
        compiler_params=_params("arbitrary"),
        name="moe_experts",
    )(te, n_used, xs, w_gu, b_gu.reshape(L, E, 1, F2), w_down, b_down.reshape(L, E, 1, D))


def _combine_kernel(pos_ref, posn_ref, gt_ref, x_ref, ys_ref, p_ref, g_ref, gw_ref, pw_ref, o_ref,
                    buf, sem):
    i = pl.program_id(0)
    tm = x_ref.shape[0]
    slot = i % 2

    def gather(rows_ref, s, wait):
        def body(j, carry):
            for k in range(TOP_K):
                cp = _row_copy(ys_ref, rows_ref[k, j], buf.at[s, k], j, sem.at[s])
                cp.wait() if wait else cp.start()
            return carry

        lax.fori_loop(0, tm, body, 0, unroll=8)

    @pl.when(i == 0)
    def _():
        gather(pos_ref, slot, False)

    @pl.when(i + 1 < pl.num_programs(0))
    def _():
        gather(posn_ref, 1 - slot, False)

    proj = _dot(p_ref[...].astype(BF16), pw_ref[...])
    gather(pos_ref, slot, True)
    gt = gt_ref[...]
    y = x_ref[...]
    for k in range(TOP_K):
        y = y + buf[slot, k] * gt[:, k:k + 1]
    h = _rms(y, g_ref[...])
    gate = jax.nn.sigmoid(_dot(h.astype(BF16), gw_ref[...]))
    o_ref[...] = y + gate * proj


def _combine_ple(x, ys, pos, gates_t, layer, p, g, gate_w, proj_w):
    T, D = x.shape
    PD = p.shape[1]
    tm = COMBINE_TM
    n = T // tm
    row = lambda i: (i, 0)
    const = lambda i: (0, 0)
    return pl.pallas_call(
        _combine_kernel,
        out_shape=jax.ShapeDtypeStruct((T, D), F32),
        grid=(n,),
        in_specs=[
            pl.BlockSpec((TOP_K, tm), lambda i: (0, i), memory_space=pltpu.SMEM),
            pl.BlockSpec((TOP_K, tm), lambda i: (0, jnp.minimum(i + 1, n - 1)), memory_space=pltpu.SMEM),
            pl.BlockSpec((tm, TOP_K), row),
            pl.BlockSpec((tm, D), row),
            pl.BlockSpec(memory_space=pl.ANY),
            pl.BlockSpec((tm, PD), lambda i: (layer * n + i, 0)),
            pl.BlockSpec((1, D), const),
            pl.BlockSpec((D, D), const),
            pl.BlockSpec((PD, D), const),
        ],
        out_specs=pl.BlockSpec((tm, D), row),
        scratch_shapes=[pltpu.VMEM((2, TOP_K, tm, D), F32), pltpu.SemaphoreType.DMA((2,))],
        compiler_params=_params("arbitrary"),
        name="moe_combine_ple",
    )(pos, pos, gates_t.T, x, ys, p, g.reshape(1, D), gate_w.astype(BF16), proj_w.astype(BF16))


def _moe_ple(x, layer, p, norm_ffn, router_w, router_b, w_gu, b_gu, w_down, b_down, ple_norm, gate_w,
             proj_w):
    T, D = x.shape
    n_rows = T * TOP_K + N_EXPERTS * EXPERT_TM
    xn, top_i, gates_t, rank, cnt = _router(x, norm_ffn, router_w, router_b)
    pos, te, n_used, ztiles = _routing_tables(top_i, rank, cnt[:, 0], EXPERT_TM, n_rows // EXPERT_TM)
    xs = _dispatch(xn, pos, ztiles, n_rows)
    ys = _experts(xs, te, n_used, layer, w_gu, b_gu, w_down, b_down)
    return _combine_ple(x, ys, pos, gates_t, layer, p, ple_norm, gate_w, proj_w)


def _trig_kernel(pos_ref, f_ref, cos_ref, sin_ref):
    ang = pos_ref[...] * f_ref[...]
    cos_ref[...] = jnp.cos(ang)
    sin_ref[...] = jnp.sin(ang)


def _rope_tables(positions):
    T = positions.size
    half = ROPE_DIM // 2
    per_row = 128 // half
    inv_freq = ROPE_THETA ** (-jnp.arange(half, dtype=F32) / half)
    pos_rep = jnp.repeat(positions.reshape(T // per_row, per_row).astype(F32), half, axis=1)
    rows = T // per_row
    tr = TRIG_ROWS
    spec = pl.BlockSpec((tr, 128), lambda i: (i, 0))
    cos, sin = pl.pallas_call(
        _trig_kernel,
        out_shape=(jax.ShapeDtypeStruct((rows, 128), F32),) * 2,
        grid=(rows // tr,),
        in_specs=[spec, pl.BlockSpec((1, 128), lambda i: (0, 0))],
        out_specs=(spec, spec),
        compiler_params=_params("arbitrary"),
        name="rope_trig",
    )(pos_rep, jnp.tile(inv_freq, per_row).reshape(1, 128))
    cos = cos.reshape(T, half)
    sin = sin.reshape(T, half)
    z = jnp.zeros_like(cos)
    c_tab = jnp.concatenate([cos, cos, z, z], axis=1)
    s_lo = jnp.concatenate([-sin, z, z, z], axis=1)
    s_hi = jnp.concatenate([z, sin, z, z], axis=1)
    return c_tab, s_lo, s_hi


def _rope(blk, c_tab, s_lo, s_hi):
    half = ROPE_DIM // 2
    return (blk * c_tab + pltpu.roll(blk, 128 - half, 1) * s_lo + pltpu.roll(blk, half, 1) * s_hi)


def _mla_pre_kernel(x_ref, g_ref, win_ref, qn_ref, wq_ref, kvn_ref, wkv_ref, gq_ref, gk_ref,
                    c_ref, sl_ref, sh_ref, q_ref, k_ref, v_ref):
    H = q_ref.shape[1]
    h = _rms(x_ref[...], g_ref[...])
    lat = _dot(h.astype(BF16), win_ref[...])
    ql = _rms(lat[:, :Q_LORA], qn_ref[...])
    kvl = _rms(lat[:, Q_LORA:Q_LORA + KV_LORA], kvn_ref[...])
    kr = lat[:, Q_LORA + KV_LORA:]
    q = _dot(ql.astype(BF16), wq_ref[...])
    kv = _dot(kvl.astype(BF16), wkv_ref[...])
    c_tab, s_lo, s_hi = c_ref[...], sl_ref[...], sh_ref[...]
    gq, gk = gq_ref[...], gk_ref[...]
    kr_ss = jnp.sum(kr * kr, axis=-1, keepdims=True)
    kr_rot = _rope(kr * gk[:, NOPE_DIM:], c_tab, s_lo, s_hi)
    qscale = QK_DIM ** -0.5 * LOG2_E
    for hd in range(H):
        q0 = q[:, hd * HEAD_PAD:hd * HEAD_PAD + NOPE_DIM]
        q1 = q[:, hd * HEAD_PAD + NOPE_DIM:(hd + 1) * HEAD_PAD]
        ss = jnp.sum(q0 * q0 + q1 * q1, axis=-1, keepdims=True)
        inv = lax.rsqrt(ss / QK_DIM + EPS) * qscale
        q_ref[0, hd, :, :NOPE_DIM] = (q0 * inv * gq[:, :NOPE_DIM]).astype(BF16)
        q_ref[0, hd, :, NOPE_DIM:] = (_rope(q1 * gq[:, NOPE_DIM:], c_tab, s_lo, s_hi) * inv).astype(BF16)
        k0 = kv[:, hd * 2 * NOPE_DIM:hd * 2 * NOPE_DIM + NOPE_DIM]
        ssk = jnp.sum(k0 * k0, axis=-1, keepdims=True) + kr_ss
        invk = lax.rsqrt(ssk / QK_DIM + EPS)
        k_ref[0, hd, :, :NOPE_DIM] = (k0 * invk * gk[:, :NOPE_DIM]).astype(BF16)
        k_ref[0, hd, :, NOPE_DIM:] = (kr_rot * invk).astype(BF16)
        v_ref[0, hd] = kv[:, hd * 2 * NOPE_DIM + NOPE_DIM:(hd + 1) * 2 * NOPE_DIM].T.astype(BF16)


def _mla_pre(x, positions, g, w_in, q_norm, w_qb, kv_norm, w_kvb, qh_norm, kh_norm):
    B, S, D = x.shape
    H = MLA_HEADS
    tm = MLA_TM
    n_lat = Q_LORA + KV_LORA
    w_in_p = jnp.concatenate([w_in, jnp.zeros((D, 128 - ROPE_DIM), F32)], axis=1).astype(BF16)
    wq = w_qb.reshape(Q_LORA, H, QK_DIM)
    wq_p = jnp.concatenate([wq, jnp.zeros((Q_LORA, H, HEAD_PAD - QK_DIM), F32)], axis=2)
    wq_p = wq_p.reshape(Q_LORA, H * HEAD_PAD).astype(BF16)
    pad_gain = lambda n: jnp.concatenate([n, jnp.zeros((HEAD_PAD - QK_DIM,), F32)]).reshape(1, HEAD_PAD)
    c_tab, s_lo, s_hi = _rope_tables(positions)
    row = lambda b, s: (b * (S // tm) + s, 0)
    const = lambda b, s: (0, 0)
    head_spec = lambda w: pl.BlockSpec((1, H, tm, w), lambda b, s: (b, 0, s, 0))
    tab_spec = pl.BlockSpec((tm, 128), row)
    return pl.pallas_call(
        _mla_pre_kernel,
        out_shape=(
            jax.ShapeDtypeStruct((B, H, S, HEAD_PAD), BF16),
            jax.ShapeDtypeStruct((B, H, S, HEAD_PAD), BF16),
            jax.ShapeDtypeStruct((B, H, V_DIM, S), BF16),
        ),
        grid=(B, S // tm),
        in_specs=[
            pl.BlockSpec((tm, D), row),
            pl.BlockSpec((1, D), const),
            pl.BlockSpec((D, n_lat + 128), const),
            pl.BlockSpec((1, Q_LORA), const),
            pl.BlockSpec((Q_LORA, H * HEAD_PAD), const),
            pl.BlockSpec((1, KV_LORA), const),
            pl.BlockSpec((KV_LORA, H * 2 * NOPE_DIM), const),
            pl.BlockSpec((1, HEAD_PAD), const),
            pl.BlockSpec((1, HEAD_PAD), const),
            tab_spec, tab_spec, tab_spec,
        ],
        out_specs=(head_spec(HEAD_PAD), head_spec(HEAD_PAD),
                   pl.BlockSpec((1, H, V_DIM, tm), lambda b, s: (b, 0, 0, s))),
        compiler_params=_params("arbitrary", "arbitrary"),
        name="mla_qkv",
    )(x.reshape(B * S, D), g.reshape(1, D), w_in_p, q_norm.reshape(1, Q_LORA), wq_p,
      kv_norm.reshape(1, KV_LORA), w_kvb.astype(BF16), pad_gain(qh_norm), pad_gain(kh_norm),
      c_tab, s_lo, s_hi)


def _attn_kernel(q_ref, k_ref, vt_ref, o_ref, m_scr, l_scr, acc_scr, st_a, st_b):
    qi = pl.program_id(2)
    tq = q_ref.shape[2]
    q = q_ref[0, 0]
    m_scr[...] = jnp.full(m_scr.shape, -jnp.inf, F32)
    l_scr[...] = jnp.zeros(l_scr.shape, F32)
    acc_scr[...] = jnp.zeros(acc_scr.shape, F32)

    def scores(ki, st_ref):
        start = pl.multiple_of(ki * tq, tq)
        k = k_ref[0, 0, pl.ds(start, tq), :]
        st_ref[...] = lax.dot_general(k, q, (((1,), (1,)), ((), ())), preferred_element_type=F32)

    def fold(ki, st_ref, diagonal):
        start = pl.multiple_of(ki * tq, tq)
        vt = vt_ref[0, 0, :, pl.ds(start, tq)]
        st = st_ref[...]
        if diagonal:
            kc = lax.broadcasted_iota(jnp.int32, (tq, tq), 0) // CHUNK
            qc = lax.broadcasted_iota(jnp.int32, (tq, tq), 1) // CHUNK
            st = jnp.where(kc <= qc, st, -jnp.inf)
        m_prev = m_scr[...]
        m_new = jnp.maximum(m_prev, jnp.max(st, axis=0, keepdims=True))
        alpha = jnp.exp2(m_prev - m_new)
        p = jnp.exp2(st - m_new)
        l_scr[...] = alpha * l_scr[...] + jnp.sum(p, axis=0, keepdims=True)
        acc_scr[...] = alpha * acc_scr[...] + _dot(vt, p.astype(BF16))
        m_scr[...] = m_new

    scores(0, st_a)

    def pair(j, carry):
        scores(2 * j + 1, st_b)
        fold(2 * j, st_a, False)
        scores(2 * j + 2, st_a)
        fold(2 * j + 1, st_b, False)
        return carry

    lax.fori_loop(0, qi // 2, pair, 0)

    @pl.when(qi % 2 == 1)
    def _():
        scores(qi, st_b)
        fold(qi - 1, st_a, False)
        fold(qi, st_b, True)

    @pl.when(qi % 2 == 0)
    def _():
        fold(qi, st_a, True)

    o_ref[0] =(acc_scr[...] / l_scr[...]).T.astype(o_ref.dtype)


def _attention(q, k, vt):
    B, H, S, _ = q.shape
    tq = ATTN_TQ
    return pl.pallas_call(
        _attn_kernel,
        out_shape=jax.ShapeDtypeStruct((B, S, H * V_DIM), BF16),
        grid=(B, H, S // tq),
        in_specs=[
            pl.BlockSpec((1, 1, tq, HEAD_PAD), lambda b, h, i: (b, h, i, 0)),
            pl.BlockSpec((1, 1, S, HEAD_PAD), lambda b, h, i: (b, h, 0, 0)),
            pl.BlockSpec((1, 1, V_DIM, S), lambda b, h, i: (b, h, 0, 0)),
        ],
        out_specs=pl.BlockSpec((1, tq, V_DIM), lambda b, h, i: (b, i, h)),
        scratch_shapes=[
            pltpu.VMEM((1, tq), F32),
            pltpu.VMEM((1, tq), F32),
            pltpu.VMEM((V_DIM, tq), F32),
            pltpu.VMEM((tq, tq), F32),
            pltpu.VMEM((tq, tq), F32),
        ],
        compiler_params=_params("arbitrary", "arbitrary", "arbitrary"),
        name="mla_attention",
    )(q, k, vt)


def _oproj_kernel(x_ref, o_ref, w_ref, out_ref):
    out_ref[...] = x_ref[...] + _dot(o_ref[...], w_ref[...])


def _oproj(x, o, w_o):
    T, D = x.shape
    tm = OPROJ_TM
    row = lambda i: (i, 0)
    return pl.pallas_call(
        _oproj_kernel,
        out_shape=jax.ShapeDtypeStruct((T, D), F32),
        grid=(T // tm,),
        in_specs=[
            pl.BlockSpec((tm, D), row),
            pl.BlockSpec((tm, o.shape[1]), row),
            pl.BlockSpec(w_o.shape, lambda i: (0, 0)),
        ],
        out_specs=pl.BlockSpec((tm, D), row),
        compiler_params=_params("arbitrary"),
        name="mla_oproj",
    )(x, o, w_o.astype(BF16))


def _mla_layer(x, positions, g, w_in, q_norm, w_qb, kv_norm, w_kvb, qh_norm, kh_norm, w_o):
    B, S, D = x.shape
    q, k, v = _mla_pre(x, positions, g, w_in, q_norm, w_qb, kv_norm, w_kvb, qh_norm, kh_norm)
    o = _attention(q, k, v)
    return _oproj(x.reshape(B * S, D), o.reshape(B * S, -1), w_o).reshape(B, S, D)


def kernel(x, p, positions, norm_mix, pool_w, pool_scale, mla_w_in, mla_q_norm, mla_w_qb, mla_kv_norm, mla_w_kvb, mla_qh_norm, mla_kh_norm, mla_w_o, norm_ffn, router_w, router_b, w_gate_up, b_gate_up, w_down, b_down, ple_norm, ple_gate_w, ple_proj_w):
    B, S, D = x.shape
    depth = norm_mix.shape[0]
    p_rows = p.reshape(depth * B * S, -1)
    for i in range(depth):
        j = i // 2
        if i % 2 == 0:
            x = _pool_layer(x, norm_mix[i], pool_w[j], pool_scale[j])
        else:
            x = _mla_layer(x, positions, norm_mix[i], mla_w_in[j], mla_q_norm[j], mla_w_qb[j],
                           mla_kv_norm[j], mla_w_kvb[j], mla_qh_norm[j], mla_kh_norm[j], mla_w_o[j])
        x = _moe_ple(x.reshape(B * S, D), i, p_rows, norm_ffn[i], router_w[i], router_b[i],
                     w_gate_up, b_gate_up, w_down, b_down, ple_norm[i], ple_gate_w[i],
                     ple_proj_w[i]).reshape(B, S, D)
    return x
```

```python
import functools

import jax
import jax.numpy as jnp
from jax import lax
from jax.experimental import pallas as pl
from jax.experimental.pallas import tpu as pltpu

F32 = jnp.float32
BF16 = jnp.bfloat16

EPS = 1e-6
POOL_WINDOWS = (2, 4, 8, 16)
POOL_HALO = 16
CHUNK = 64
MLA_HEADS = 8
Q_LORA = 384
KV_LORA = 256
NOPE_DIM = 128
ROPE_DIM = 64
V_DIM = 128
QK_DIM = NOPE_DIM + ROPE_DIM
HEAD_PAD = 256
ROPE_THETA = 10000.0
LOG2_E = 1.4426950408889634
N_EXPERTS = 32
TOP_K = 4
SWIGLU_LIMIT = 7.0
SWIGLU_ALPHA = 1.702

V7X_VMEM_LIMIT = 56 * 1024 * 1024

POOL_TS = 512
ROUTER_TM = 512
DISPATCH_TM = 512
EXPERT_TM = 512
EXPERT_FC = 512
COMBINE_TM = 256
MLA_TM = 512
ATTN_TQ = 512
OPROJ_TM = 512
TRIG_ROWS = 512


def _params(*sem):
    return pltpu.CompilerParams(dimension_semantics=sem, vmem_limit_bytes=V7X_VMEM_LIMIT)


def _rms(x, g):
    return x * lax.rsqrt(jnp.mean(x * x, axis=-1, keepdims=True) + EPS) * g


def _dot(a, b):
    return jnp.dot(a, b, preferred_element_type=F32)


SLAB = 8
LANES = 128


def _slab_load(ref, n):
    return jnp.concatenate([ref[pl.ds(s, n, stride=SLAB), :] for s in range(SLAB)], axis=-1)


def _slab_store(ref, x):
    n = x.shape[0]
    for s in range(SLAB):
        ref[pl.ds(s, n, stride=SLAB), :] = x[:, s * LANES:(s + 1) * LANES]


def _pool_kernel(x_ref, g_ref, w_ref, sc_ref, o_ref, hbuf):
    s = pl.program_id(1)
    ts = x_ref.shape[1]
    gd = w_ref.shape[1]
    x = x_ref[0]
    h = _rms(x, g_ref[...])

    @pl.when(s == 0)
    def _():
        hbuf[0:POOL_HALO, :] = jnp.zeros((POOL_HALO, x.shape[1]), F32)

    hbuf[POOL_HALO:, :] = h
    pos1 = s * ts + lax.broadcasted_iota(jnp.int32, (ts, 1), 0) + 1
    outs = []
    for g, win in enumerate(POOL_WINDOWS):
        lo, hi = g * gd, (g + 1) * gd
        hg = h[:, lo:hi]
        acc = hg
        for j in range(1, win):
            acc = acc + hbuf[POOL_HALO - j:POOL_HALO - j + ts, lo:hi]
        cnt = jnp.minimum(pos1, win).astype(F32)
        diff = acc / cnt - hg
        outs.append(_dot(diff.astype(BF16), w_ref[g]))
    mix = jnp.concatenate(outs, axis=-1) * sc_ref[...]
    o_ref[0] = x + mix
    hbuf[0:POOL_HALO, :] = hbuf[ts:ts + POOL_HALO, :]


def _pool_layer(x, g, w, scale):
    B, S, D = x.shape
    ts = POOL_TS
    G, gd, _ = w.shape
    return pl.pallas_call(
        _pool_kernel,
        out_shape=jax.ShapeDtypeStruct((B, S, D), F32),
        grid=(B, S // ts),
        in_specs=[
            pl.BlockSpec((1, ts, D), lambda b, s: (b, s, 0)),
            pl.BlockSpec((1, D), lambda b, s: (0, 0)),
            pl.BlockSpec((G, gd, gd), lambda b, s: (0, 0, 0)),
            pl.BlockSpec((1, D), lambda b, s: (0, 0)),
        ],
        out_specs=pl.BlockSpec((1, ts, D), lambda b, s: (b, s, 0)),
        scratch_shapes=[pltpu.VMEM((ts + POOL_HALO, D), F32)],
        compiler_params=_params("arbitrary", "arbitrary"),
        name="pool_mixer",
    )(x, g.reshape(1, D), w.astype(BF16), scale.reshape(1, D))


def _router_kernel(x_ref, g_ref, rwt_ref, rb_ref, xn_ref, ti_ref, gt_ref, rk_ref, cnt_ref, run):
    i = pl.program_id(0)
    tm = x_ref.shape[0]
    E = rwt_ref.shape[0]

    @pl.when(i == 0)
    def _():
        run[...] = jnp.zeros(run.shape, F32)

    xn = _rms(x_ref[...], g_ref[...])
    _slab_store(xn_ref, xn)
    logits =lax.dot_general(rwt_ref[...], xn, (((1,), (1,)), ((), ())),
                             precision=lax.Precision.HIGHEST,
                             preferred_element_type=F32) + rb_ref[...]
    eidx = lax.broadcasted_iota(jnp.int32, (E, tm), 0)
    vals, sels = [], []
    l = logits
    for k in range(TOP_K):
        m = jnp.max(l, axis=0, keepdims=True)
        idx = jnp.min(jnp.where(l == m, eidx, E), axis=0, keepdims=True)
        sel = eidx == idx
        ti_ref[k:k + 1, :] = idx
        vals.append(m)
        sels.append(sel)
        l = jnp.where(sel, -jnp.inf, l)
    ex = [jnp.exp(v - vals[0]) for v in vals]
    den = ex[0] + ex[1] + ex[2] + ex[3]
    for k in range(TOP_K):
        gt_ref[k:k + 1, :] = ex[k] / den
    member = jnp.zeros((E, tm), F32)
    for sel in sels:
        member = member + sel.astype(F32)
    r = lax.broadcasted_iota(jnp.int32, (tm, tm), 0)
    c = lax.broadcasted_iota(jnp.int32, (tm, tm), 1)
    before = jnp.where(r < c, 1.0, 0.0).astype(BF16)
    prior = _dot(member.astype(BF16), before) + run[:, 0:1]
    for k in range(TOP_K):
        rk = jnp.sum(jnp.where(sels[k], prior, 0.0), axis=0, keepdims=True)
        rk_ref[k:k + 1, :] = rk.astype(jnp.int32)
    run[...] = run[...] + jnp.sum(member, axis=1, keepdims=True)
    cnt_ref[...] = run[...]


def _router(x, g, rw, rb):
    T, D = x.shape
    E = rw.shape[1]
    tm = ROUTER_TM
    row = lambda i: (i, 0)
    col = lambda i: (0, i)
    const = lambda i: (0, 0)
    return pl.pallas_call(
        _router_kernel,
        out_shape=(
            jax.ShapeDtypeStruct((T * SLAB, LANES), F32),
            jax.ShapeDtypeStruct((TOP_K, T), jnp.int32),
            jax.ShapeDtypeStruct((TOP_K, T), F32),
            jax.ShapeDtypeStruct((TOP_K, T), jnp.int32),
            jax.ShapeDtypeStruct((E, 128), F32),
        ),
        grid=(T // tm,),
        in_specs=[
            pl.BlockSpec((tm, D), row),
            pl.BlockSpec((1, D), const),
            pl.BlockSpec((E, D), const),
            pl.BlockSpec((E, 1), const),
        ],
        out_specs=(
            pl.BlockSpec((tm * SLAB, LANES), row),
            pl.BlockSpec((TOP_K, tm), col),
            pl.BlockSpec((TOP_K, tm), col),
            pl.BlockSpec((TOP_K, tm), col),
            pl.BlockSpec((E, 128), const),
        ),
        scratch_shapes=[pltpu.VMEM((E, 128), F32)],
        compiler_params=_params("arbitrary"),
        name="moe_router",
    )(x, g.reshape(1, D), rw.T, rb.reshape(E, 1))


def _routing_tables(top_i, rank, counts, tm, n_tiles):
    counts = counts.astype(jnp.int32)
    padded = ((counts + tm - 1) // tm) * tm
    ends = jnp.cumsum(padded)
    offs = ends - padded
    e_ids = jnp.arange(N_EXPERTS, dtype=jnp.int32)[:, None, None]
    pos = rank + jnp.sum(jnp.where(top_i[None] == e_ids, offs[:, None, None], 0), axis=0)
    n_used = ends[-1] // tm
    tiles = jnp.arange(n_tiles, dtype=jnp.int32)
    te = jnp.sum((ends[None, :] <= (tiles * tm)[:, None]).astype(jnp.int32), axis=1)
    te = jnp.minimum(te, N_EXPERTS - 1)
    te = jnp.where(tiles < n_used, te, jnp.take(te, n_used - 1))
    cand = jnp.concatenate([ends // tm - 1, tiles])
    valid = jnp.concatenate([padded > 0, tiles >= n_used])
    order = jnp.argsort(jnp.logical_not(valid), stable=True)
    ztiles = jnp.concatenate([jnp.take(cand, order), jnp.sum(valid.astype(jnp.int32)).reshape(1)])
    return (pos, te.astype(jnp.int32), n_used.reshape(1).astype(jnp.int32), ztiles.astype(jnp.int32))


def _row_copy(src_ref, src_row, dst_ref, dst_row, sem):
    src = src_ref.at[pl.ds(pl.multiple_of(src_row * SLAB, SLAB), SLAB)]
    dst = dst_ref.at[pl.ds(pl.multiple_of(dst_row * SLAB, SLAB), SLAB)]
    return pltpu.make_async_copy(src, dst, sem)


def _dispatch_kernel(zt_ref, pos_ref, xn_ref, xs_ref, zbuf, sem):
    tm = pos_ref.shape[1]
    zt = zbuf.shape[0]

    @pl.when(pl.program_id(0) == 0)
    def _():
        zbuf[...] = jnp.zeros(zbuf.shape, F32)
        nz = zt_ref[zt_ref.shape[0] - 1]

        def zero_copy(t):
            row = pl.multiple_of(zt_ref[t] * zt, zt)
            return pltpu.make_async_copy(zbuf, xs_ref.at[pl.ds(row, zt)], sem)

        def zstart(t, carry):
            zero_copy(t).start()
            return carry

        def zwait(t, carry):
            zero_copy(t).wait()
            return carry

        lax.fori_loop(0, nz, zstart, 0)
        lax.fori_loop(0, nz, zwait, 0)

    def issue(j, carry):
        for k in range(TOP_K):
            _row_copy(xn_ref, j, xs_ref, pos_ref[k, j], sem).start(priority=k % 2)
        return carry

    lax.fori_loop(0, tm, issue, 0, unroll=8)

    def drain(j, carry):
        for k in range(TOP_K):
            _row_copy(xn_ref, j, xs_ref, pos_ref[k, j], sem).wait()
        return carry

    lax.fori_loop(0, tm, drain, 0, unroll=8)


def _dispatch(xn, pos, ztiles, n_rows):
    T = xn.shape[0] // SLAB
    tm = DISPATCH_TM
    return pl.pallas_call(
        _dispatch_kernel,
        out_shape=jax.ShapeDtypeStruct((n_rows * SLAB, LANES), F32),
        grid_spec=pltpu.PrefetchScalarGridSpec(
            num_scalar_prefetch=1,
            grid=(T // tm,),
            in_specs=[
                pl.BlockSpec((TOP_K, tm), lambda i, zt: (0, i), memory_space=pltpu.SMEM),
                pl.BlockSpec((tm * SLAB, LANES), lambda i, zt: (i, 0)),
            ],
            out_specs=pl.BlockSpec(memory_space=pl.ANY),
            scratch_shapes=[pltpu.VMEM((EXPERT_TM * SLAB, LANES), F32), pltpu.SemaphoreType.DMA],
        ),
        compiler_params=_params("arbitrary"),
        name="moe_dispatch",
    )(ztiles, pos, xn)


def _expert_kernel(te_ref, nu_ref, xs_ref, wgu_ref, bgu_ref, wd_ref, bd_ref, ys_ref,
                   wgu_bf, wd_bf, act):
    i = pl.program_id(0)
    F = wd_ref.shape[1]
    e = te_ref[i]
    prev = te_ref[jnp.maximum(i - 1, 0)]
    active = i < nu_ref[0]
    new_expert = jnp.logical_or(i == 0, e != prev)

    @pl.when(jnp.logical_and(active, new_expert))
    def _():
        wgu_bf[...] = wgu_ref[0].astype(BF16)
        wd_bf[...] = wd_ref[0].astype(BF16)

    @pl.when(active)
    def _():
        x = _slab_load(xs_ref, act.shape[0]).astype(BF16)
        fc = EXPERT_FC
        for c in range(F // fc):
            gate = _dot(x, wgu_bf[:, c * fc:(c + 1) * fc]) + bgu_ref[0, :, c * fc:(c + 1) * fc]
            up = _dot(x, wgu_bf[:, F + c * fc:F + (c + 1) * fc]) + bgu_ref[0, :, F + c * fc:F + (c + 1) * fc]
            gate = jnp.minimum(gate, SWIGLU_LIMIT)
            up = jnp.clip(up, -SWIGLU_LIMIT, SWIGLU_LIMIT)
            a = (up + 1.0) * gate * jax.nn.sigmoid(SWIGLU_ALPHA * gate)
            act[:, c * fc:(c + 1) * fc] = a.astype(BF16)
        _slab_store(ys_ref, _dot(act[...], wd_bf[...]) + bd_ref[0])

    @pl.when(jnp.logical_not(active))
    def _():
        ys_ref[...] = jnp.zeros(ys_ref.shape, F32)


def _experts(xs, te, n_used, layer, w_gu, b_gu, w_down, b_down):
    P = xs.shape[0] // SLAB
    L, E, D, F2 = w_gu.shape
    F = F2 // 2
    tm = EXPERT_TM
    n_tiles = P // tm
    tile = lambda i, te, nu: (jnp.minimum(i, nu[0] - 1), 0)
    wsel = lambda i, te, nu: (layer, te[i], 0, 0)
    return pl.pallas_call(
        _expert_kernel,
        out_shape=jax.ShapeDtypeStruct((P * SLAB, LANES), F32),
        grid_spec=pltpu.PrefetchScalarGridSpec(
            num_scalar_prefetch=2,
            grid=(n_tiles,),
            in_specs=[
                pl.BlockSpec((tm * SLAB, LANES), tile),
                pl.BlockSpec((None, 1, D, F2), wsel),
                pl.BlockSpec((None, 1, 1, F2), wsel),
                pl.BlockSpec((None, 1, F, D), wsel),
                pl.BlockSpec((None, 1, 1, D), wsel),
            ],
            out_specs=pl.BlockSpec((tm * SLAB, LANES), lambda i, te, nu: (i, 0)),
            scratch_shapes=[
                pltpu.VMEM((D, F2), BF16),
                pltpu.VMEM((F, D), BF16),
                pltpu.VMEM((tm, F), BF16),
            ],
        ),
        compiler_params=_params("arbitrary"),
        name="moe_experts",
    )(te, n_used, xs, w_gu, b_gu.reshape(L, E, 1, F2), w_down, b_down.reshape(L, E, 1, D))


def _combine_kernel(pos_ref, posn_ref, gt_ref, x_ref, ys_ref, p_ref, g_ref, gw_ref, pw_ref, o_ref,
                    buf, sem):
    i = pl.program_id(0)
    tm = x_ref.shape[0]
    slot = i % 2

    def gather(rows_ref, s, wait):
        def body(j, carry):
            for k in range(TOP_K):
                cp = _row_copy(ys_ref, rows_ref[k, j], buf.at[s, k], j, sem.at[s])
                cp.wait() if wait else cp.start(priority=k % 2)
            return carry

        lax.fori_loop(0, tm, body, 0, unroll=8)

    @pl.when(i == 0)
    def _():
        gather(pos_ref, slot, False)

    @pl.when(i + 1 < pl.num_programs(0))
    def _():
        gather(posn_ref, 1 - slot, False)

    proj = _dot(p_ref[...].astype(BF16), pw_ref[...])
    gather(pos_ref, slot, True)
    gt = gt_ref[...]
    y = x_ref[...]
    for k in range(TOP_K):
        y = y + _slab_load(buf.at[slot, k], tm) * gt[:, k:k + 1]
    h = _rms(y, g_ref[...])
    gate = jax.nn.sigmoid(_dot(h.astype(BF16), gw_ref[...]))
    o_ref[...] = y + gate * proj


def _combine_ple(x, ys, pos, gates_t, layer, p, g, gate_w, proj_w):
    T, D = x.shape
    PD = p.shape[1]
    tm = COMBINE_TM
    n = T // tm
    row = lambda i: (i, 0)
    const = lambda i: (0, 0)
    return pl.pallas_call(
        _combine_kernel,
        out_shape=jax.ShapeDtypeStruct((T, D), F32),
        grid=(n,),
        in_specs=[
            pl.BlockSpec((TOP_K, tm), lambda i: (0, i), memory_space=pltpu.SMEM),
            pl.BlockSpec((TOP_K, tm), lambda i: (0, jnp.minimum(i + 1, n - 1)), memory_space=pltpu.SMEM),
            pl.BlockSpec((tm, TOP_K), row),
            pl.BlockSpec((tm, D), row),
            pl.BlockSpec(memory_space=pl.ANY),
            pl.BlockSpec((tm, PD), lambda i: (layer * n + i, 0)),
            pl.BlockSpec((1, D), const),
            pl.BlockSpec((D, D), const),
            pl.BlockSpec((PD, D), const),
        ],
        out_specs=pl.BlockSpec((tm, D), row),
        scratch_shapes=[pltpu.VMEM((2, TOP_K, tm * SLAB, LANES), F32), pltpu.SemaphoreType.DMA((2,))],
        compiler_params=_params("arbitrary"),
        name="moe_combine_ple",
    )(pos, pos, gates_t.T, x, ys, p, g.reshape(1, D), gate_w.astype(BF16), proj_w.astype(BF16))


def _moe_ple(x, layer, p, norm_ffn, router_w, router_b, w_gu, b_gu, w_down, b_down, ple_norm, gate_w,
             proj_w):
    T, D = x.shape
    n_rows = T * TOP_K + N_EXPERTS * EXPERT_TM
    xn, top_i, gates_t, rank, cnt = _router(x, norm_ffn, router_w, router_b)
    pos, te, n_used, ztiles = _routing_tables(top_i, rank, cnt[:, 0], EXPERT_TM, n_rows // EXPERT_TM)
    xs = _dispatch(xn, pos, ztiles, n_rows)
    ys = _experts(xs, te, n_used, layer, w_gu, b_gu, w_down, b_down)
    return _combine_ple(x, ys, pos, gates_t, layer, p, ple_norm, gate_w, proj_w)


def _trig_kernel(pos_ref, f_ref, cos_ref, sin_ref):
    ang = pos_ref[...] * f_ref[...]
    cos_ref[...] = jnp.cos(ang)
    sin_ref[...] = jnp.sin(ang)


def _rope_tables(positions):
    T = positions.size
    half = ROPE_DIM // 2
    per_row = 128 // half
    inv_freq = ROPE_THETA ** (-jnp.arange(half, dtype=F32) / half)
    pos_rep = jnp.repeat(positions.reshape(T // per_row, per_row).astype(F32), half, axis=1)
    rows = T // per_row
    tr = TRIG_ROWS
    spec = pl.BlockSpec((tr, 128), lambda i: (i, 0))
    cos, sin = pl.pallas_call(
        _trig_kernel,
        out_shape=(jax.ShapeDtypeStruct((rows, 128), F32),) * 2,
        grid=(rows // tr,),
        in_specs=[spec, pl.BlockSpec((1, 128), lambda i: (0, 0))],
        out_specs=(spec, spec),
        compiler_params=_params("arbitrary"),
        name="rope_trig",
    )(pos_rep, jnp.tile(inv_freq, per_row).reshape(1, 128))
    cos = cos.reshape(T, half)
    sin = sin.reshape(T, half)
    z = jnp.zeros_like(cos)
    c_tab = jnp.concatenate([cos, cos, z, z], axis=1)
    s_lo = jnp.concatenate([-sin, z, z, z], axis=1)
    s_hi = jnp.concatenate([z, sin, z, z], axis=1)
    return c_tab, s_lo, s_hi


def _rope(blk, c_tab, s_lo, s_hi):
    half = ROPE_DIM // 2
    return (blk * c_tab + pltpu.roll(blk, 128 - half, 1) * s_lo + pltpu.roll(blk, half, 1) * s_hi)


def _mla_pre_kernel(x_ref, g_ref, win_ref, qn_ref, wq_ref, kvn_ref, wkv_ref, gq_ref, gk_ref,
                    c_ref, sl_ref, sh_ref, q_ref, k_ref, v_ref):
    H = q_ref.shape[1]
    h = _rms(x_ref[...], g_ref[...])
    lat = _dot(h.astype(BF16), win_ref[...])
    ql = _rms(lat[:, :Q_LORA], qn_ref[...])
    kvl = _rms(lat[:, Q_LORA:Q_LORA + KV_LORA], kvn_ref[...])
    kr = lat[:, Q_LORA + KV_LORA:]
    q = _dot(ql.astype(BF16), wq_ref[...])
    kv = _dot(kvl.astype(BF16), wkv_ref[...])
    c_tab, s_lo, s_hi = c_ref[...], sl_ref[...], sh_ref[...]
    gq, gk = gq_ref[...], gk_ref[...]
    kr_sq = kr * kr
    kr_rot = _rope(kr * gk[:, NOPE_DIM:], c_tab, s_lo, s_hi)
    qscale = QK_DIM ** -0.5 * LOG2_E
    ones = jnp.ones((NOPE_DIM, NOPE_DIM), BF16)
    for hd in range(H):
        q0 = q[:, hd * HEAD_PAD:hd * HEAD_PAD + NOPE_DIM]
        q1 = q[:, hd * HEAD_PAD + NOPE_DIM:(hd + 1) * HEAD_PAD]
        ss = _dot((q0 * q0 + q1 * q1).astype(BF16), ones)
        inv = lax.rsqrt(ss / QK_DIM + EPS) * qscale
        q_ref[0, hd, :, :NOPE_DIM] = (q0 * inv * gq[:, :NOPE_DIM]).astype(BF16)
        q_ref[0, hd, :, NOPE_DIM:] = (_rope(q1 * gq[:, NOPE_DIM:], c_tab, s_lo, s_hi) * inv).astype(BF16)
        k0 = kv[:, hd * 2 * NOPE_DIM:hd * 2 * NOPE_DIM + NOPE_DIM]
        ssk = _dot((k0 * k0 + kr_sq).astype(BF16), ones)
        invk = lax.rsqrt(ssk / QK_DIM + EPS)
        k_ref[0, hd, :, :NOPE_DIM] = (k0 * invk * gk[:, :NOPE_DIM]).astype(BF16)
        k_ref[0, hd, :, NOPE_DIM:] = (kr_rot * invk).astype(BF16)
        v_ref[0, hd] = kv[:, hd * 2 * NOPE_DIM + NOPE_DIM:(hd + 1) * 2 * NOPE_DIM].T.astype(BF16)


def _mla_pre(x, positions, g, w_in, q_norm, w_qb, kv_norm, w_kvb, qh_norm, kh_norm):
    B, S, D = x.shape
    H = MLA_HEADS
    tm = MLA_TM
    n_lat = Q_LORA + KV_LORA
    w_in_p = jnp.concatenate([w_in, jnp.zeros((D, 128 - ROPE_DIM), F32)], axis=1).astype(BF16)
    wq = w_qb.reshape(Q_LORA, H, QK_DIM)
    wq_p = jnp.concatenate([wq, jnp.zeros((Q_LORA, H, HEAD_PAD - QK_DIM), F32)], axis=2)
    wq_p = wq_p.reshape(Q_LORA, H * HEAD_PAD).astype(BF16)
    pad_gain = lambda n: jnp.concatenate([n, jnp.zeros((HEAD_PAD - QK_DIM,), F32)]).reshape(1, HEAD_PAD)
    c_tab, s_lo, s_hi = _rope_tables(positions)
    row = lambda b, s: (b * (S // tm) + s, 0)
    const = lambda b, s: (0, 0)
    head_spec = lambda w: pl.BlockSpec((1, H, tm, w), lambda b, s: (b, 0, s, 0))
    tab_spec = pl.BlockSpec((tm, 128), row)
    return pl.pallas_call(
        _mla_pre_kernel,
        out_shape=(
            jax.ShapeDtypeStruct((B, H, S, HEAD_PAD), BF16),
            jax.ShapeDtypeStruct((B, H, S, HEAD_PAD), BF16),
            jax.ShapeDtypeStruct((B, H, V_DIM, S), BF16),
        ),
        grid=(B, S // tm),
        in_specs=[
            pl.BlockSpec((tm, D), row),
            pl.BlockSpec((1, D), const),
            pl.BlockSpec((D, n_lat + 128), const),
            pl.BlockSpec((1, Q_LORA), const),
            pl.BlockSpec((Q_LORA, H * HEAD_PAD), const),
            pl.BlockSpec((1, KV_LORA), const),
            pl.BlockSpec((KV_LORA, H * 2 * NOPE_DIM), const),
            pl.BlockSpec((1, HEAD_PAD), const),
            pl.BlockSpec((1, HEAD_PAD), const),
            tab_spec, tab_spec, tab_spec,
        ],
        out_specs=(head_spec(HEAD_PAD), head_spec(HEAD_PAD),
                   pl.BlockSpec((1, H, V_DIM, tm), lambda b, s: (b, 0, 0, s))),
        compiler_params=_params("arbitrary", "arbitrary"),
        name="mla_qkv",
    )(x.reshape(B * S, D), g.reshape(1, D), w_in_p, q_norm.reshape(1, Q_LORA), wq_p,
      kv_norm.reshape(1, KV_LORA), w_kvb.astype(BF16), pad_gain(qh_norm), pad_gain(kh_norm),
      c_tab, s_lo, s_hi)


def _attn_kernel(q_ref, k_ref, vt_ref, o_ref, m_scr, l_scr, acc_scr, st_a, st_b):
    qi = pl.program_id(2)
    tq = q_ref.shape[2]
    q = q_ref[0, 0]
    m_scr[...] = jnp.full(m_scr.shape, -jnp.inf, F32)
    l_scr[...] = jnp.zeros(l_scr.shape, F32)
    acc_scr[...] = jnp.zeros(acc_scr.shape, F32)

    def scores(ki, st_ref):
        start = pl.multiple_of(ki * tq, tq)
        k = k_ref[0, 0, pl.ds(start, tq), :]
        st_ref[...] = lax.dot_general(k, q, (((1,), (1,)), ((), ())), preferred_element_type=F32)

    def fold(ki, st_ref, diagonal):
        start = pl.multiple_of(ki * tq, tq)
        vt = vt_ref[0, 0, :, pl.ds(start, tq)]
        st = st_ref[...]
        if diagonal:
            kc = lax.broadcasted_iota(jnp.int32, (tq, tq), 0) // CHUNK
            qc = lax.broadcasted_iota(jnp.int32, (tq, tq), 1) // CHUNK
            st = jnp.where(kc <= qc, st, -jnp.inf)
        m_prev = m_scr[...]
        m_new = jnp.maximum(m_prev, jnp.max(st, axis=0, keepdims=True))
        alpha = jnp.exp2(m_prev - m_new)
        p = jnp.exp2(st - m_new)
        l_scr[...] = alpha * l_scr[...] + jnp.sum(p, axis=0, keepdims=True)
        acc_scr[...] = alpha * acc_scr[...] + _dot(vt, p.astype(BF16))
        m_scr[...] = m_new

    scores(0, st_a)

    def pair(j, carry):
        scores(2 * j + 1, st_b)
        fold(2 * j, st_a, False)
        scores(2 * j + 2, st_a)
        fold(2 * j + 1, st_b, False)
        return carry

    lax.fori_loop(0, qi // 2, pair, 0)

    @pl.when(qi % 2 == 1)
    def _():
        scores(qi, st_b)
        fold(qi - 1, st_a, False)
        fold(qi, st_b, True)

    @pl.when(qi % 2 == 0)
    def _():
        fold(qi, st_a, True)

    o_ref[0] =(acc_scr[...] / l_scr[...]).T.astype(o_ref.dtype)


def _attention(q, k, vt):
    B, H, S, _ = q.shape
    tq = ATTN_TQ
    return pl.pallas_call(
        _attn_kernel,
        out_shape=jax.ShapeDtypeStruct((B, S, H * V_DIM), BF16),
        grid=(B, H, S // tq),
        in_specs=[
            pl.BlockSpec((1, 1, tq, HEAD_PAD), lambda b, h, i: (b, h, i, 0)),
            pl.BlockSpec((1, 1, S, HEAD_PAD), lambda b, h, i: (b, h, 0, 0)),
            pl.BlockSpec((1, 1, V_DIM, S), lambda b, h, i: (b, h, 0, 0)),
        ],
        out_specs=pl.BlockSpec((1, tq, V_DIM), lambda b, h, i: (b, i, h)),
        scratch_shapes=[
            pltpu.VMEM((1, tq), F32),
            pltpu.VMEM((1, tq), F32),
            pltpu.VMEM((V_DIM, tq), F32),
            pltpu.VMEM((tq, tq), F32),
            pltpu.VMEM((tq, tq), F32),
        ],
        compiler_params=_params("arbitrary", "arbitrary", "arbitrary"),
        name="mla_attention",
    )(q, k, vt)


def _oproj_kernel(x_ref, o_ref, w_ref, out_ref):
    out_ref[...] = x_ref[...] + _dot(o_ref[...], w_ref[...])


def _oproj(x, o, w_o):
    T, D = x.shape
    tm = OPROJ_TM
    row = lambda i: (i, 0)
    return pl.pallas_call(
        _oproj_kernel,
        out_shape=jax.ShapeDtypeStruct((T, D), F32),
        grid=(T // tm,),
        in_specs=[
            pl.BlockSpec((tm, D), row),
            pl.BlockSpec((tm, o.shape[1]), row),
            pl.BlockSpec(w_o.shape, lambda i: (0, 0)),
        ],
        out_specs=pl.BlockSpec((tm, D), row),
        compiler_params=_params("arbitrary"),
        name="mla_oproj",
    )(x, o, w_o.astype(BF16))


def _mla_layer(x, positions, g, w_in, q_norm, w_qb, kv_norm, w_kvb, qh_norm, kh_norm, w_o):
    B, S, D = x.shape
    q, k, v = _mla_pre(x, positions, g, w_in, q_norm, w_qb, kv_norm, w_kvb, qh_norm, kh_norm)
    o = _attention(q, k, v)
    return _oproj(x.reshape(B * S, D), o.reshape(B * S, -1), w_o).reshape(B, S, D)


def kernel(x, p, positions, norm_mix, pool_w, pool_scale, mla_w_in, mla_q_norm, mla_w_qb, mla_kv_norm, mla_w_kvb, mla_qh_norm, mla_kh_norm, mla_w_o, norm_ffn, router_w, router_b, w_gate_up, b_gate_up, w_down, b_down, ple_norm, ple_gate_w, ple_proj_w):
    B, S, D = x.shape
    depth = norm_mix.shape[0]
    p_rows = p.reshape(depth * B * S, -1)
    for i in range(depth):
        j = i // 2
        if i % 2 == 0:
            x = _pool_layer(x, norm_mix[i], pool_w[j], pool_scale[j])
        else:
            x = _mla_layer(x, positions, norm_mix[i], mla_w_in[j], mla_q_norm[j], mla_w_qb[j],
                           mla_kv_norm[j], mla_w_kvb[j], mla_qh_norm[j], mla_kh_norm[j], mla_w_o[j])
        x = _moe_ple(x.reshape(B * S, D), i, p_rows, norm_ffn[i], router_w[i], router_b[i],
                     w_gate_up, b_gate_up, w_down, b_down, ple_norm[i], ple_gate_w[i],
                     ple_proj_w[i]).reshape(B, S, D)
    return x
```

```python
import functools

import jax
import jax.numpy as jnp
from jax import lax
from jax.experimental import pallas as pl
from jax.experimental.pallas import tpu as pltpu

F32 = jnp.float32
BF16 = jnp.bfloat16

EPS = 1e-6
POOL_WINDOWS = (2, 4, 8, 16)
POOL_HALO = 16
CHUNK = 64
MLA_HEADS = 8
Q_LORA = 384
KV_LORA = 256
NOPE_DIM = 128
ROPE_DIM = 64
V_DIM = 128
QK_DIM = NOPE_DIM + ROPE_DIM
HEAD_PAD = 256
ROPE_THETA = 10000.0
LOG2_E = 1.4426950408889634
N_EXPERTS = 32
TOP_K = 4
SWIGLU_LIMIT = 7.0
SWIGLU_ALPHA = 1.702

V7X_VMEM_LIMIT = 56 * 1024 * 1024

POOL_TS = 512
ROUTER_TM = 512
DISPATCH_TM = 512
EXPERT_TM = 512
EXPERT_FC = 512
COMBINE_TM = 256
MLA_TM = 512
ATTN_TQ = 512
OPROJ_TM = 512
TRIG_ROWS = 512


def _params(*sem):
    return pltpu.CompilerParams(dimension_semantics=sem, vmem_limit_bytes=V7X_VMEM_LIMIT)


def _rms(x, g):
    return x * lax.rsqrt(jnp.mean(x * x, axis=-1, keepdims=True) + EPS) * g


def _dot(a, b):
    return jnp.dot(a, b, preferred_element_type=F32)


SLAB = 8
LANES = 128


def _slab_load(ref, n):
    return jnp.concatenate([ref[pl.ds(s, n, stride=SLAB), :] for s in range(SLAB)], axis=-1)


def _slab_store(ref, x):
    n = x.shape[0]
    for s in range(SLAB):
        ref[pl.ds(s, n, stride=SLAB), :] = x[:, s * LANES:(s + 1) * LANES]


def _pool_kernel(x_ref, g_ref, w_ref, sc_ref, o_ref, hbuf):
    s = pl.program_id(1)
    ts = x_ref.shape[1]
    gd = w_ref.shape[1]
    x = x_ref[0]
    h = _rms(x, g_ref[...])

    @pl.when(s == 0)
    def _():
        hbuf[0:POOL_HALO, :] = jnp.zeros((POOL_HALO, x.shape[1]), F32)

    hbuf[POOL_HALO:, :] = h
    pos1 = s * ts + lax.broadcasted_iota(jnp.int32, (ts, 1), 0) + 1
    outs = []
    for g, win in enumerate(POOL_WINDOWS):
        lo, hi = g * gd, (g + 1) * gd
        hg = h[:, lo:hi]
        acc = hg
        for j in range(1, win):
            acc = acc + hbuf[POOL_HALO - j:POOL_HALO - j + ts, lo:hi]
        cnt = jnp.minimum(pos1, win).astype(F32)
        diff = acc / cnt - hg
        outs.append(_dot(diff.astype(BF16), w_ref[g]))
    mix = jnp.concatenate(outs, axis=-1) * sc_ref[...]
    o_ref[0] = x + mix
    hbuf[0:POOL_HALO, :] = hbuf[ts:ts + POOL_HALO, :]


def _pool_layer(x, g, w, scale):
    B, S, D = x.shape
    ts = POOL_TS
    G, gd, _ = w.shape
    return pl.pallas_call(
        _pool_kernel,
        out_shape=jax.ShapeDtypeStruct((B, S, D), F32),
        grid=(B, S // ts),
        in_specs=[
            pl.BlockSpec((1, ts, D), lambda b, s: (b, s, 0)),
            pl.BlockSpec((1, D), lambda b, s: (0, 0)),
            pl.BlockSpec((G, gd, gd), lambda b, s: (0, 0, 0)),
            pl.BlockSpec((1, D), lambda b, s: (0, 0)),
        ],
        out_specs=pl.BlockSpec((1, ts, D), lambda b, s: (b, s, 0)),
        scratch_shapes=[pltpu.VMEM((ts + POOL_HALO, D), F32)],
        compiler_params=_params("arbitrary", "arbitrary"),
        name="pool_mixer",
    )(x, g.reshape(1, D), w.astype(BF16), scale.reshape(1, D))


def _router_kernel(x_ref, g_ref, rwt_ref, rb_ref, xn_ref, ti_ref, gt_ref, rk_ref, cnt_ref, run):
    i = pl.program_id(0)
    tm = x_ref.shape[0]
    E = rwt_ref.shape[0]

    @pl.when(i == 0)
    def _():
        run[...] = jnp.zeros(run.shape, F32)

    xn = _rms(x_ref[...], g_ref[...])
    _slab_store(xn_ref, xn)
    logits =lax.dot_general(rwt_ref[...], xn, (((1,), (1,)), ((), ())),
                             precision=lax.Precision.HIGHEST,
                             preferred_element_type=F32) + rb_ref[...]
    eidx = lax.broadcasted_iota(jnp.int32, (E, tm), 0)
    vals, sels = [], []
    l = logits
    for k in range(TOP_K):
        m = jnp.max(l, axis=0, keepdims=True)
        idx = jnp.min(jnp.where(l == m, eidx, E), axis=0, keepdims=True)
        sel = eidx == idx
        ti_ref[k:k + 1, :] = idx
        vals.append(m)
        sels.append(sel)
        l = jnp.where(sel, -jnp.inf, l)
    ex = [jnp.exp(v - vals[0]) for v in vals]
    den = ex[0] + ex[1] + ex[2] + ex[3]
    for k in range(TOP_K):
        gt_ref[k:k + 1, :] = ex[k] / den
    member = jnp.zeros((E, tm), F32)
    for sel in sels:
        member = member + sel.astype(F32)
    r = lax.broadcasted_iota(jnp.int32, (tm, tm), 0)
    c = lax.broadcasted_iota(jnp.int32, (tm, tm), 1)
    before = jnp.where(r < c, 1.0, 0.0).astype(BF16)
    prior = _dot(member.astype(BF16), before) + run[:, 0:1]
    for k in range(TOP_K):
        rk = jnp.sum(jnp.where(sels[k], prior, 0.0), axis=0, keepdims=True)
        rk_ref[k:k + 1, :] = rk.astype(jnp.int32)
    run[...] = run[...] + jnp.sum(member, axis=1, keepdims=True)
    cnt_ref[...] = run[...]


def _router(x, g, rw, rb):
    T, D = x.shape
    E = rw.shape[1]
    tm = ROUTER_TM
    row = lambda i: (i, 0)
    col = lambda i: (0, i)
    const = lambda i: (0, 0)
    return pl.pallas_call(
        _router_kernel,
        out_shape=(
            jax.ShapeDtypeStruct((T * SLAB, LANES), F32),
            jax.ShapeDtypeStruct((TOP_K, T), jnp.int32),
            jax.ShapeDtypeStruct((TOP_K, T), F32),
            jax.ShapeDtypeStruct((TOP_K, T), jnp.int32),
            jax.ShapeDtypeStruct((E, 128), F32),
        ),
        grid=(T // tm,),
        in_specs=[
            pl.BlockSpec((tm, D), row),
            pl.BlockSpec((1, D), const),
            pl.BlockSpec((E, D), const),
            pl.BlockSpec((E, 1), const),
        ],
        out_specs=(
            pl.BlockSpec((tm * SLAB, LANES), row),
            pl.BlockSpec((TOP_K, tm), col),
            pl.BlockSpec((TOP_K, tm), col),
            pl.BlockSpec((TOP_K, tm), col),
            pl.BlockSpec((E, 128), const),
        ),
        scratch_shapes=[pltpu.VMEM((E, 128), F32)],
        compiler_params=_params("arbitrary"),
        name="moe_router",
    )(x, g.reshape(1, D), rw.T, rb.reshape(E, 1))


def _routing_tables(top_i, rank, counts, tm, n_tiles):
    counts = counts.astype(jnp.int32)
    padded = ((counts + tm - 1) // tm) * tm
    ends = jnp.cumsum(padded)
    offs = ends - padded
    e_ids = jnp.arange(N_EXPERTS, dtype=jnp.int32)[:, None, None]
    pos = rank + jnp.sum(jnp.where(top_i[None] == e_ids, offs[:, None, None], 0), axis=0)
    n_used = ends[-1] // tm
    tiles = jnp.arange(n_tiles, dtype=jnp.int32)
    te = jnp.sum((ends[None, :] <= (tiles * tm)[:, None]).astype(jnp.int32), axis=1)
    te = jnp.minimum(te, N_EXPERTS - 1)
    te = jnp.where(tiles < n_used, te, jnp.take(te, n_used - 1))
    first = jnp.logical_and(tiles < n_used, te != jnp.concatenate([jnp.full((1,), -1, jnp.int32), te[:-1]]))
    slot = (jnp.cumsum(first.astype(jnp.int32)) - 1) % 2
    nxt_tile = jnp.take(ends, te) // tm
    nxt = jnp.where(nxt_tile < n_used, jnp.take(te, jnp.minimum(nxt_tile, n_tiles - 1)), -1)
    sched = jnp.stack([te, first.astype(jnp.int32), slot, nxt]).astype(jnp.int32)
    cand = jnp.concatenate([ends // tm - 1, tiles])
    valid = jnp.concatenate([padded > 0, tiles >= n_used])
    order = jnp.argsort(jnp.logical_not(valid), stable=True)
    ztiles = jnp.concatenate([jnp.take(cand, order), jnp.sum(valid.astype(jnp.int32)).reshape(1)])
    return pos, sched, n_used.reshape(1).astype(jnp.int32), ztiles.astype(jnp.int32)


def _row_copy(src_ref, src_row, dst_ref, dst_row, sem):
    src = src_ref.at[pl.ds(pl.multiple_of(src_row * SLAB, SLAB), SLAB)]
    dst = dst_ref.at[pl.ds(pl.multiple_of(dst_row * SLAB, SLAB), SLAB)]
    return pltpu.make_async_copy(src, dst, sem)


def _dispatch_kernel(zt_ref, pos_ref, xn_ref, xs_ref, zbuf, sem):
    tm = pos_ref.shape[1]
    zt = zbuf.shape[0]

    @pl.when(pl.program_id(0) == 0)
    def _():
        zbuf[...] = jnp.zeros(zbuf.shape, F32)
        nz = zt_ref[zt_ref.shape[0] - 1]

        def zero_copy(t):
            row = pl.multiple_of(zt_ref[t] * zt, zt)
            return pltpu.make_async_copy(zbuf, xs_ref.at[pl.ds(row, zt)], sem)

        def zstart(t, carry):
            zero_copy(t).start()
            return carry

        def zwait(t, carry):
            zero_copy(t).wait()
            return carry

        lax.fori_loop(0, nz, zstart, 0)
        lax.fori_loop(0, nz, zwait, 0)

    def issue(j, carry):
        for k in range(TOP_K):
            _row_copy(xn_ref, j, xs_ref, pos_ref[k, j], sem).start(priority=k % 2)
        return carry

    lax.fori_loop(0, tm, issue, 0, unroll=8)

    def drain(j, carry):
        for k in range(TOP_K):
            _row_copy(xn_ref, j, xs_ref, pos_ref[k, j], sem).wait()
        return carry

    lax.fori_loop(0, tm, drain, 0, unroll=8)


def _dispatch(xn, pos, ztiles, n_rows):
    T = xn.shape[0] // SLAB
    tm = DISPATCH_TM
    return pl.pallas_call(
        _dispatch_kernel,
        out_shape=jax.ShapeDtypeStruct((n_rows * SLAB, LANES), F32),
        grid_spec=pltpu.PrefetchScalarGridSpec(
            num_scalar_prefetch=1,
            grid=(T // tm,),
            in_specs=[
                pl.BlockSpec((TOP_K, tm), lambda i, zt: (0, i), memory_space=pltpu.SMEM),
                pl.BlockSpec((tm * SLAB, LANES), lambda i, zt: (i, 0)),
            ],
            out_specs=pl.BlockSpec(memory_space=pl.ANY),
            scratch_shapes=[pltpu.VMEM((EXPERT_TM * SLAB, LANES), F32), pltpu.SemaphoreType.DMA],
        ),
        compiler_params=_params("arbitrary"),
        name="moe_dispatch",
    )(ztiles, pos, xn)


SCHED_EXPERT, SCHED_FIRST, SCHED_SLOT, SCHED_NEXT = range(4)


def _expert_kernel(layer, sched_ref, nu_ref, xs_ref, wgu_hbm, bgu_ref, wd_hbm, bd_ref, ys_ref,
                   wgu_f32, wd_f32, wgu_bf, wd_bf, act, sem):
    i = pl.program_id(0)
    F = wd_bf.shape[0]
    e = sched_ref[SCHED_EXPERT, i]
    slot = sched_ref[SCHED_SLOT, i]
    nxt = sched_ref[SCHED_NEXT, i]
    active = i < nu_ref[0]

    def weight_copies(expert, s):
        return (pltpu.make_async_copy(wgu_hbm.at[layer, expert], wgu_f32.at[s], sem.at[0, s]),
                pltpu.make_async_copy(wd_hbm.at[layer, expert], wd_f32.at[s], sem.at[1, s]))

    @pl.when(jnp.logical_and(active, sched_ref[SCHED_FIRST, i] == 1))
    def _():
        @pl.when(i == 0)
        def _():
            for cp in weight_copies(e, slot):
                cp.start()

        @pl.when(nxt >= 0)
        def _():
            for cp in weight_copies(nxt, 1 - slot):
                cp.start()

        for cp in weight_copies(e, slot):
            cp.wait()
        wgu_bf[...] = wgu_f32[slot].astype(BF16)
        wd_bf[...] = wd_f32[slot].astype(BF16)

    @pl.when(active)
    def _():
        x = _slab_load(xs_ref, act.shape[0]).astype(BF16)
        fc = EXPERT_FC
        for c in range(F // fc):
            gate = _dot(x, wgu_bf[:, c * fc:(c + 1) * fc]) + bgu_ref[0, :, c * fc:(c + 1) * fc]
            up = _dot(x, wgu_bf[:, F + c * fc:F + (c + 1) * fc]) + bgu_ref[0, :, F + c * fc:F + (c + 1) * fc]
            gate = jnp.minimum(gate, SWIGLU_LIMIT)
            up = jnp.clip(up, -SWIGLU_LIMIT, SWIGLU_LIMIT)
            a = (up + 1.0) * gate * jax.nn.sigmoid(SWIGLU_ALPHA * gate)
            act[:, c * fc:(c + 1) * fc] = a.astype(BF16)
        _slab_store(ys_ref, _dot(act[...], wd_bf[...]) + bd_ref[0])

    @pl.when(jnp.logical_not(active))
    def _():
        ys_ref[...] = jnp.zeros(ys_ref.shape, F32)


def _experts(xs, sched, n_used, layer, w_gu, b_gu, w_down, b_down):
    P = xs.shape[0] // SLAB
    L, E, D, F2 = w_gu.shape
    F = F2 // 2
    tm = EXPERT_TM
    n_tiles = P // tm
    tile = lambda i, sc, nu: (jnp.minimum(i, nu[0] - 1), 0)
    bsel = lambda i, sc, nu: (layer, sc[SCHED_EXPERT, i], 0, 0)
    return pl.pallas_call(
        functools.partial(_expert_kernel, layer),
        out_shape=jax.ShapeDtypeStruct((P * SLAB, LANES), F32),
        grid_spec=pltpu.PrefetchScalarGridSpec(
            num_scalar_prefetch=2,
            grid=(n_tiles,),
            in_specs=[
                pl.BlockSpec((tm * SLAB, LANES), tile),
                pl.BlockSpec(memory_space=pl.ANY),
                pl.BlockSpec((None, 1, 1, F2), bsel),
                pl.BlockSpec(memory_space=pl.ANY),
                pl.BlockSpec((None, 1, 1, D), bsel),
            ],
            out_specs=pl.BlockSpec((tm * SLAB, LANES), lambda i, sc, nu: (i, 0)),
            scratch_shapes=[
                pltpu.VMEM((2, D, F2), F32),
                pltpu.VMEM((2, F, D), F32),
                pltpu.VMEM((D, F2), BF16),
                pltpu.VMEM((F, D), BF16),
                pltpu.VMEM((tm, F), BF16),
                pltpu.SemaphoreType.DMA((2, 2)),
            ],
        ),
        compiler_params=_params("arbitrary"),
        name="moe_experts",
    )(sched, n_used, xs, w_gu, b_gu.reshape(L, E, 1, F2), w_down, b_down.reshape(L, E, 1, D))


def _combine_kernel(pos_ref, posn_ref, gt_ref, x_ref, ys_ref, p_ref, g_ref, gw_ref, pw_ref, o_ref,
                    buf_a, buf_b, sem):
    i = pl.program_id(0)
    last = pl.num_programs(0) - 1
    tm = x_ref.shape[0]

    def gather_loop(rows_ref, dst, s, wait):
        def body(j, carry):
            for k in range(TOP_K):
                cp = _row_copy(ys_ref, rows_ref[k, j], dst.at[k], j, sem.at[s])
                cp.wait() if wait else cp.start(priority=k % 2)
            return carry

        lax.fori_loop(0, tm, body, 0, unroll=8)

    @pl.when(i == 0)
    def _():
        gather_loop(pos_ref, buf_a, 0, False)

    def step(cur, cur_s, nxt, nxt_s):
        gather_loop(pos_ref, cur, cur_s, True)
        for j in range(tm):
            for k in range(TOP_K):
                _row_copy(ys_ref, posn_ref[k, j], nxt.at[k], j, sem.at[nxt_s]).start(priority=k % 2)
        proj = _dot(p_ref[...].astype(BF16), pw_ref[...])
        gt = gt_ref[...]
        y = x_ref[...]
        for k in range(TOP_K):
            y = y + _slab_load(cur.at[k], tm) * gt[:, k:k + 1]
        h = _rms(y, g_ref[...])
        gate = jax.nn.sigmoid(_dot(h.astype(BF16), gw_ref[...]))
        o_ref[...] = y + gate * proj

        @pl.when(i == last)
        def _():
            gather_loop(posn_ref, nxt, nxt_s, True)

    @pl.when(i % 2 == 0)
    def _():
        step(buf_a, 0, buf_b, 1)

    @pl.when(i % 2 == 1)
    def _():
        step(buf_b, 1, buf_a, 0)


def _combine_ple(x, ys, pos, gates_t, layer, p, g, gate_w, proj_w):
    T, D = x.shape
    PD = p.shape[1]
    tm = COMBINE_TM
    n = T // tm
    row = lambda i: (i, 0)
    const = lambda i: (0, 0)
    return pl.pallas_call(
        _combine_kernel,
        out_shape=jax.ShapeDtypeStruct((T, D), F32),
        grid=(n,),
        in_specs=[
            pl.BlockSpec((TOP_K, tm), lambda i: (0, i), memory_space=pltpu.SMEM),
            pl.BlockSpec((TOP_K, tm), lambda i: (0, jnp.minimum(i + 1, n - 1)), memory_space=pltpu.SMEM),
            pl.BlockSpec((tm, TOP_K), row),
            pl.BlockSpec((tm, D), row),
            pl.BlockSpec(memory_space=pl.ANY),
            pl.BlockSpec((tm, PD), lambda i: (layer * n + i, 0)),
            pl.BlockSpec((1, D), const),
            pl.BlockSpec((D, D), const),
            pl.BlockSpec((PD, D), const),
        ],
        out_specs=pl.BlockSpec((tm, D), row),
        scratch_shapes=[pltpu.VMEM((TOP_K, tm * SLAB, LANES), F32),
                        pltpu.VMEM((TOP_K, tm * SLAB, LANES), F32),
                        pltpu.SemaphoreType.DMA((2,))],
        compiler_params=_params("arbitrary"),
        name="moe_combine_ple",
    )(pos, pos, gates_t.T, x, ys, p, g.reshape(1, D), gate_w.astype(BF16), proj_w.astype(BF16))


def _moe_ple(x, layer, p, norm_ffn, router_w, router_b, w_gu, b_gu, w_down, b_down, ple_norm, gate_w,
             proj_w):
    T, D = x.shape
    n_rows = T * TOP_K + N_EXPERTS * EXPERT_TM
    xn, top_i, gates_t, rank, cnt = _router(x, norm_ffn, router_w, router_b)
    pos, sched, n_used, ztiles = _routing_tables(top_i, rank, cnt[:, 0], EXPERT_TM, n_rows // EXPERT_TM)
    xs = _dispatch(xn, pos, ztiles, n_rows)
    ys = _experts(xs, sched, n_used, layer, w_gu, b_gu, w_down, b_down)
    return _combine_ple(x, ys, pos, gates_t, layer, p, ple_norm, gate_w, proj_w)


def _trig_kernel(pos_ref, f_ref, cos_ref, sin_ref):
    ang = pos_ref[...] * f_ref[...]
    cos_ref[...] = jnp.cos(ang)
    sin_ref[...] = jnp.sin(ang)


def _rope_tables(positions):
    T = positions.size
    half = ROPE_DIM // 2
    per_row = 128 // half
    inv_freq = ROPE_THETA ** (-jnp.arange(half, dtype=F32) / half)
    pos_rep = jnp.repeat(positions.reshape(T // per_row, per_row).astype(F32), half, axis=1)
    rows = T // per_row
    tr = TRIG_ROWS
    spec = pl.BlockSpec((tr, 128), lambda i: (i, 0))
    cos, sin = pl.pallas_call(
        _trig_kernel,
        out_shape=(jax.ShapeDtypeStruct((rows, 128), F32),) * 2,
        grid=(rows // tr,),
        in_specs=[spec, pl.BlockSpec((1, 128), lambda i: (0, 0))],
        out_specs=(spec, spec),
        compiler_params=_params("arbitrary"),
        name="rope_trig",
    )(pos_rep, jnp.tile(inv_freq, per_row).reshape(1, 128))
    cos = cos.reshape(T, half)
    sin = sin.reshape(T, half)
    z = jnp.zeros_like(cos)
    c_tab = jnp.concatenate([cos, cos, z, z], axis=1)
    s_lo = jnp.concatenate([-sin, z, z, z], axis=1)
    s_hi = jnp.concatenate([z, sin, z, z], axis=1)
    return c_tab, s_lo, s_hi


def _rope(blk, c_tab, s_lo, s_hi):
    half = ROPE_DIM // 2
    return (blk * c_tab + pltpu.roll(blk, 128 - half, 1) * s_lo + pltpu.roll(blk, half, 1) * s_hi)


def _mla_pre_kernel(x_ref, g_ref, win_ref, qn_ref, wq_ref, kvn_ref, wkv_ref, gq_ref, gk_ref,
                    c_ref, sl_ref, sh_ref, q_ref, k_ref, v_ref):
    H = q_ref.shape[1]
    h = _rms(x_ref[...], g_ref[...])
    lat = _dot(h.astype(BF16), win_ref[...])
    ql = _rms(lat[:, :Q_LORA], qn_ref[...])
    kvl = _rms(lat[:, Q_LORA:Q_LORA + KV_LORA], kvn_ref[...])
    kr = lat[:, Q_LORA + KV_LORA:]
    q = _dot(ql.astype(BF16), wq_ref[...])
    kv = _dot(kvl.astype(BF16), wkv_ref[...])
    c_tab, s_lo, s_hi = c_ref[...], sl_ref[...], sh_ref[...]
    gq, gk = gq_ref[...], gk_ref[...]
    kr_sq = kr * kr
    kr_rot = _rope(kr * gk[:, NOPE_DIM:], c_tab, s_lo, s_hi)
    qscale = QK_DIM ** -0.5 * LOG2_E
    ones = jnp.ones((NOPE_DIM, NOPE_DIM), BF16)
    for hd in range(H):
        q0 = q[:, hd * HEAD_PAD:hd * HEAD_PAD + NOPE_DIM]
        q1 = q[:, hd * HEAD_PAD + NOPE_DIM:(hd + 1) * HEAD_PAD]
        ss = _dot((q0 * q0 + q1 * q1).astype(BF16), ones)
        inv = lax.rsqrt(ss / QK_DIM + EPS) * qscale
        q_ref[0, hd, :, :NOPE_DIM] = (q0 * inv * gq[:, :NOPE_DIM]).astype(BF16)
        q_ref[0, hd, :, NOPE_DIM:] = (_rope(q1 * gq[:, NOPE_DIM:], c_tab, s_lo, s_hi) * inv).astype(BF16)
        k0 = kv[:, hd * 2 * NOPE_DIM:hd * 2 * NOPE_DIM + NOPE_DIM]
        ssk = _dot((k0 * k0 + kr_sq).astype(BF16), ones)
        invk = lax.rsqrt(ssk / QK_DIM + EPS)
        k_ref[0, hd, :, :NOPE_DIM] = (k0 * invk * gk[:, :NOPE_DIM]).astype(BF16)
        k_ref[0, hd, :, NOPE_DIM:] = (kr_rot * invk).astype(BF16)
        v_ref[0, hd] = kv[:, hd * 2 * NOPE_DIM + NOPE_DIM:(hd + 1) * 2 * NOPE_DIM].T.astype(BF16)


def _mla_pre(x, positions, g, w_in, q_norm, w_qb, kv_norm, w_kvb, qh_norm, kh_norm):
    B, S, D = x.shape
    H = MLA_HEADS
    tm = MLA_TM
    n_lat = Q_LORA + KV_LORA
    w_in_p = jnp.concatenate([w_in, jnp.zeros((D, 128 - ROPE_DIM), F32)], axis=1).astype(BF16)
    wq = w_qb.reshape(Q_LORA, H, QK_DIM)
    wq_p = jnp.concatenate([wq, jnp.zeros((Q_LORA, H, HEAD_PAD - QK_DIM), F32)], axis=2)
    wq_p = wq_p.reshape(Q_LORA, H * HEAD_PAD).astype(BF16)
    pad_gain = lambda n: jnp.concatenate([n, jnp.zeros((HEAD_PAD - QK_DIM,), F32)]).reshape(1, HEAD_PAD)
    c_tab, s_lo, s_hi = _rope_tables(positions)
    row = lambda b, s: (b * (S // tm) + s, 0)
    const = lambda b, s: (0, 0)
    head_spec = lambda w: pl.BlockSpec((1, H, tm, w), lambda b, s: (b, 0, s, 0))
    tab_spec = pl.BlockSpec((tm, 128), row)
    return pl.pallas_call(
        _mla_pre_kernel,
        out_shape=(
            jax.ShapeDtypeStruct((B, H, S, HEAD_PAD), BF16),
            jax.ShapeDtypeStruct((B, H, S, HEAD_PAD), BF16),
            jax.ShapeDtypeStruct((B, H, V_DIM, S), BF16),
        ),
        grid=(B, S // tm),
        in_specs=[
            pl.BlockSpec((tm, D), row),
            pl.BlockSpec((1, D), const),
            pl.BlockSpec((D, n_lat + 128), const),
            pl.BlockSpec((1, Q_LORA), const),
            pl.BlockSpec((Q_LORA, H * HEAD_PAD), const),
            pl.BlockSpec((1, KV_LORA), const),
            pl.BlockSpec((KV_LORA, H * 2 * NOPE_DIM), const),
            pl.BlockSpec((1, HEAD_PAD), const),
            pl.BlockSpec((1, HEAD_PAD), const),
            tab_spec, tab_spec, tab_spec,
        ],
        out_specs=(head_spec(HEAD_PAD), head_spec(HEAD_PAD),
                   pl.BlockSpec((1, H, V_DIM, tm), lambda b, s: (b, 0, 0, s))),
        compiler_params=_params("arbitrary", "arbitrary"),
        name="mla_qkv",
    )(x.reshape(B * S, D), g.reshape(1, D), w_in_p, q_norm.reshape(1, Q_LORA), wq_p,
      kv_norm.reshape(1, KV_LORA), w_kvb.astype(BF16), pad_gain(qh_norm), pad_gain(kh_norm),
      c_tab, s_lo, s_hi)


def _attn_kernel(q_ref, k_ref, vt_ref, o_ref, m_scr, l_scr, acc_scr, st_a, st_b):
    qi = pl.program_id(2)
    tq = q_ref.shape[2]
    q = q_ref[0, 0]
    m_scr[...] = jnp.full(m_scr.shape, -jnp.inf, F32)
    l_scr[...] = jnp.zeros(l_scr.shape, F32)
    acc_scr[...] = jnp.zeros(acc_scr.shape, F32)

    def scores(ki, st_ref):
        start = pl.multiple_of(ki * tq, tq)
        k = k_ref[0, 0, pl.ds(start, tq), :]
        st_ref[...] = lax.dot_general(k, q, (((1,), (1,)), ((), ())), preferred_element_type=F32)

    def fold(ki, st_ref, diagonal):
        start = pl.multiple_of(ki * tq, tq)
        vt = vt_ref[0, 0, :, pl.ds(start, tq)]
        st = st_ref[...]
        if diagonal:
            kc = lax.broadcasted_iota(jnp.int32, (tq, tq), 0) // CHUNK
            qc = lax.broadcasted_iota(jnp.int32, (tq, tq), 1) // CHUNK
            st = jnp.where(kc <= qc, st, -jnp.inf)
        m_prev = m_scr[...]
        m_new = jnp.maximum(m_prev, jnp.max(st, axis=0, keepdims=True))
        alpha = jnp.exp2(m_prev - m_new)
        p = jnp.exp2(st - m_new)
        l_scr[...] = alpha * l_scr[...] + jnp.sum(p, axis=0, keepdims=True)
        acc_scr[...] = alpha * acc_scr[...] + _dot(vt, p.astype(BF16))
        m_scr[...] = m_new

    scores(0, st_a)

    def pair(j, carry):
        scores(2 * j + 1, st_b)
        fold(2 * j, st_a, False)
        scores(2 * j + 2, st_a)
        fold(2 * j + 1, st_b, False)
        return carry

    lax.fori_loop(0, qi // 2, pair, 0)

    @pl.when(qi % 2 == 1)
    def _():
        scores(qi, st_b)
        fold(qi - 1, st_a, False)
        fold(qi, st_b, True)

    @pl.when(qi % 2 == 0)
    def _():
        fold(qi, st_a, True)

    o_ref[0] =(acc_scr[...] / l_scr[...]).T.astype(o_ref.dtype)


def _attention(q, k, vt):
    B, H, S, _ = q.shape
    tq = ATTN_TQ
    return pl.pallas_call(
        _attn_kernel,
        out_shape=jax.ShapeDtypeStruct((B, S, H * V_DIM), BF16),
        grid=(B, H, S // tq),
        in_specs=[
            pl.BlockSpec((1, 1, tq, HEAD_PAD), lambda b, h, i: (b, h, i, 0)),
            pl.BlockSpec((1, 1, S, HEAD_PAD), lambda b, h, i: (b, h, 0, 0)),
            pl.BlockSpec((1, 1, V_DIM, S), lambda b, h, i: (b, h, 0, 0)),
        ],
        out_specs=pl.BlockSpec((1, tq, V_DIM), lambda b, h, i: (b, i, h)),
        scratch_shapes=[
            pltpu.VMEM((1, tq), F32),
            pltpu.VMEM((1, tq), F32),
            pltpu.VMEM((V_DIM, tq), F32),
            pltpu.VMEM((tq, tq), F32),
            pltpu.VMEM((tq, tq), F32),
        ],
        compiler_params=_params("arbitrary", "arbitrary", "arbitrary"),
        name="mla_attention",
    )(q, k, vt)


def _oproj_kernel(x_ref, o_ref, w_ref, out_ref):
    out_ref[...] = x_ref[...] + _dot(o_ref[...], w_ref[...])


def _oproj(x, o, w_o):
    T, D = x.shape
    tm = OPROJ_TM
    row = lambda i: (i, 0)
    return pl.pallas_call(
        _oproj_kernel,
        out_shape=jax.ShapeDtypeStruct((T, D), F32),
        grid=(T // tm,),
        in_specs=[
            pl.BlockSpec((tm, D), row),
            pl.BlockSpec((tm, o.shape[1]), row),
            pl.BlockSpec(w_o.shape, lambda i: (0, 0)),
        ],
        out_specs=pl.BlockSpec((tm, D), row),
        compiler_params=_params("arbitrary"),
        name="mla_oproj",
    )(x, o, w_o.astype(BF16))


def _mla_layer(x, positions, g, w_in, q_norm, w_qb, kv_norm, w_kvb, qh_norm, kh_norm, w_o):
    B, S, D = x.shape
    q, k, v = _mla_pre(x, positions, g, w_in, q_norm, w_qb, kv_norm, w_kvb, qh_norm, kh_norm)
    o = _attention(q, k, v)
    return _oproj(x.reshape(B * S, D), o.reshape(B * S, -1), w_o).reshape(B, S, D)


def kernel(x, p, positions, norm_mix, pool_w, pool_scale, mla_w_in, mla_q_norm, mla_w_qb, mla_kv_norm, mla_w_kvb, mla_qh_norm, mla_kh_norm, mla_w_o, norm_ffn, router_w, router_b, w_gate_up, b_gate_up, w_down, b_down, ple_norm, ple_gate_w, ple_proj_w):
    B, S, D = x.shape
    depth = norm_mix.shape[0]
    p_rows = p.reshape(depth * B * S, -1)
    for i in range(depth):
        j = i // 2
        if i % 2 == 0:
            x = _pool_layer(x, norm_mix[i], pool_w[j], pool_scale[j])
        else:
            x = _mla_layer(x, positions, norm_mix[i], mla_w_in[j], mla_q_norm[j], mla_w_qb[j],
                           mla_kv_norm[j], mla_w_kvb[j], mla_qh_norm[j], mla_kh_norm[j], mla_w_o[j])
        x = _moe_ple(x.reshape(B * S, D), i, p_rows, norm_ffn[i], router_w[i], router_b[i],
                     w_gate_up, b_gate_up, w_down, b_down, ple_norm[i], ple_gate_w[i],
                     ple_proj_w[i]).reshape(B, S, D)
    return x
```

```python
import functools

import jax
import jax.numpy as jnp
from jax import lax
from jax.experimental import pallas as pl
from jax.experimental.pallas import tpu as pltpu

F32 = jnp.float32
BF16 = jnp.bfloat16

EPS = 1e-6
POOL_WINDOWS = (2, 4, 8, 16)
POOL_HALO = 16
CHUNK = 64
MLA_HEADS = 8
Q_LORA = 384
KV_LORA = 256
NOPE_DIM = 128
ROPE_DIM = 64
V_DIM = 128
QK_DIM = NOPE_DIM + ROPE_DIM
HEAD_PAD = 256
ROPE_THETA = 10000.0
LOG2_E = 1.4426950408889634
N_EXPERTS = 32
TOP_K = 4
SWIGLU_LIMIT = 7.0
SWIGLU_ALPHA = 1.702

V7X_VMEM_LIMIT = 56 * 1024 * 1024

POOL_TS = 512
ROUTER_TM = 512
EXPERT_TM = 512
EXPERT_FC = 512
COMBINE_TM = 256
MLA_TM = 512
ATTN_TQ = 512
OPROJ_TM = 512
TRIG_ROWS = 512


def _params(*sem):
    return pltpu.CompilerParams(dimension_semantics=sem, vmem_limit_bytes=V7X_VMEM_LIMIT)


def _rms(x, g):
    return x * lax.rsqrt(jnp.mean(x * x, axis=-1, keepdims=True) + EPS) * g


def _dot(a, b):
    return jnp.dot(a, b, preferred_element_type=F32)


SLAB = 8
LANES = 128


def _slab_load(ref, n):
    return jnp.concatenate([ref[pl.ds(s, n, stride=SLAB), :] for s in range(SLAB)], axis=-1)


def _slab_store(ref, x):
    n = x.shape[0]
    for s in range(SLAB):
        ref[pl.ds(s, n, stride=SLAB), :] = x[:, s * LANES:(s + 1) * LANES]


def _pool_kernel(x_ref, g_ref, w_ref, sc_ref, o_ref, hbuf):
    s = pl.program_id(1)
    ts = x_ref.shape[1]
    gd = w_ref.shape[1]
    x = x_ref[0]
    h = _rms(x, g_ref[...])

    @pl.when(s == 0)
    def _():
        hbuf[0:POOL_HALO, :] = jnp.zeros((POOL_HALO, x.shape[1]), F32)

    hbuf[POOL_HALO:, :] = h
    pos1 = s * ts + lax.broadcasted_iota(jnp.int32, (ts, 1), 0) + 1
    outs = []
    for g, win in enumerate(POOL_WINDOWS):
        lo, hi = g * gd, (g + 1) * gd
        hg = h[:, lo:hi]
        acc = hg
        for j in range(1, win):
            acc = acc + hbuf[POOL_HALO - j:POOL_HALO - j + ts, lo:hi]
        cnt = jnp.minimum(pos1, win).astype(F32)
        diff = acc / cnt - hg
        outs.append(_dot(diff.astype(BF16), w_ref[g]))
    mix = jnp.concatenate(outs, axis=-1) * sc_ref[...]
    o_ref[0] = x + mix
    hbuf[0:POOL_HALO, :] = hbuf[ts:ts + POOL_HALO, :]


def _pool_layer(x, g, w, scale):
    B, S, D = x.shape
    ts = POOL_TS
    G, gd, _ = w.shape
    return pl.pallas_call(
        _pool_kernel,
        out_shape=jax.ShapeDtypeStruct((B, S, D), F32),
        grid=(B, S // ts),
        in_specs=[
            pl.BlockSpec((1, ts, D), lambda b, s: (b, s, 0)),
            pl.BlockSpec((1, D), lambda b, s: (0, 0)),
            pl.BlockSpec((G, gd, gd), lambda b, s: (0, 0, 0)),
            pl.BlockSpec((1, D), lambda b, s: (0, 0)),
        ],
        out_specs=pl.BlockSpec((1, ts, D), lambda b, s: (b, s, 0)),
        scratch_shapes=[pltpu.VMEM((ts + POOL_HALO, D), F32)],
        compiler_params=_params("arbitrary", "arbitrary"),
        name="pool_mixer",
    )(x, g.reshape(1, D), w.astype(BF16), scale.reshape(1, D))


def _router_kernel(x_ref, g_ref, rwt_ref, rb_ref, xn_ref, ti_ref, gt_ref, rk_ref, cnt_ref, run):
    i = pl.program_id(0)
    tm = x_ref.shape[0]
    E = rwt_ref.shape[0]

    @pl.when(i == 0)
    def _():
        run[...] = jnp.zeros(run.shape, F32)

    xn = _rms(x_ref[...], g_ref[...])
    _slab_store(xn_ref, xn)
    logits =lax.dot_general(rwt_ref[...], xn, (((1,), (1,)), ((), ())),
                             precision=lax.Precision.HIGHEST,
                             preferred_element_type=F32) + rb_ref[...]
    eidx = lax.broadcasted_iota(jnp.int32, (E, tm), 0)
    vals, sels = [], []
    l = logits
    for k in range(TOP_K):
        m = jnp.max(l, axis=0, keepdims=True)
        idx = jnp.min(jnp.where(l == m, eidx, E), axis=0, keepdims=True)
        sel = eidx == idx
        ti_ref[k:k + 1, :] = idx
        vals.append(m)
        sels.append(sel)
        l = jnp.where(sel, -jnp.inf, l)
    ex = [jnp.exp(v - vals[0]) for v in vals]
    den = ex[0] + ex[1] + ex[2] + ex[3]
    for k in range(TOP_K):
        gt_ref[k:k + 1, :] = ex[k] / den
    member = jnp.zeros((E, tm), F32)
    for sel in sels:
        member = member + sel.astype(F32)
    r = lax.broadcasted_iota(jnp.int32, (tm, tm), 0)
    c = lax.broadcasted_iota(jnp.int32, (tm, tm), 1)
    before = jnp.where(r < c, 1.0, 0.0).astype(BF16)
    prior = _dot(member.astype(BF16), before) + run[:, 0:1]
    for k in range(TOP_K):
        rk = jnp.sum(jnp.where(sels[k], prior, 0.0), axis=0, keepdims=True)
        rk_ref[k:k + 1, :] = rk.astype(jnp.int32)
    run[...] = run[...] + jnp.sum(member, axis=1, keepdims=True)
    cnt_ref[...] = run[...]


def _router(x, g, rw, rb):
    T, D = x.shape
    E = rw.shape[1]
    tm = ROUTER_TM
    row = lambda i: (i, 0)
    col = lambda i: (0, i)
    const = lambda i: (0, 0)
    return pl.pallas_call(
        _router_kernel,
        out_shape=(
            jax.ShapeDtypeStruct((T * SLAB, LANES), F32),
            jax.ShapeDtypeStruct((TOP_K, T), jnp.int32),
            jax.ShapeDtypeStruct((TOP_K, T), F32),
            jax.ShapeDtypeStruct((TOP_K, T), jnp.int32),
            jax.ShapeDtypeStruct((E, 128), F32),
        ),
        grid=(T // tm,),
        in_specs=[
            pl.BlockSpec((tm, D), row),
            pl.BlockSpec((1, D), const),
            pl.BlockSpec((E, D), const),
            pl.BlockSpec((E, 1), const),
        ],
        out_specs=(
            pl.BlockSpec((tm * SLAB, LANES), row),
            pl.BlockSpec((TOP_K, tm), col),
            pl.BlockSpec((TOP_K, tm), col),
            pl.BlockSpec((TOP_K, tm), col),
            pl.BlockSpec((E, 128), const),
        ),
        scratch_shapes=[pltpu.VMEM((E, 128), F32)],
        compiler_params=_params("arbitrary"),
        name="moe_router",
    )(x, g.reshape(1, D), rw.T, rb.reshape(E, 1))


def _routing_tables(top_i, rank, counts, tm, n_tiles):
    counts = counts.astype(jnp.int32)
    padded = ((counts + tm - 1) // tm) * tm
    ends = jnp.cumsum(padded)
    offs = ends - padded
    e_ids = jnp.arange(N_EXPERTS, dtype=jnp.int32)[:, None, None]
    pos = rank + jnp.sum(jnp.where(top_i[None] == e_ids, offs[:, None, None], 0), axis=0)
    n_used = ends[-1] // tm
    tiles = jnp.arange(n_tiles, dtype=jnp.int32)
    te = jnp.sum((ends[None, :] <= (tiles * tm)[:, None]).astype(jnp.int32), axis=1)
    te = jnp.minimum(te, N_EXPERTS - 1)
    te = jnp.where(tiles < n_used, te, jnp.take(te, n_used - 1))
    first = jnp.logical_and(tiles < n_used, te != jnp.concatenate([jnp.full((1,), -1, jnp.int32), te[:-1]]))
    slot = (jnp.cumsum(first.astype(jnp.int32)) - 1) % 2
    nxt_tile = jnp.take(ends, te) // tm
    nxt = jnp.where(nxt_tile < n_used, jnp.take(te, jnp.minimum(nxt_tile, n_tiles - 1)), -1)
    sched = jnp.stack([te, first.astype(jnp.int32), slot, nxt]).astype(jnp.int32)
    T = top_i.shape[1]
    by_row = jnp.argsort(pos.reshape(-1)).astype(jnp.int32) % T
    by_row = jnp.concatenate([by_row, jnp.zeros((tm,), jnp.int32)])
    unpadded_offs = jnp.cumsum(counts) - counts
    window = tiles * tm - jnp.take(offs, te) + jnp.take(unpadded_offs, te)
    window = jnp.clip(window, 0, TOP_K * T)
    tok = jax.vmap(lambda s: lax.dynamic_slice(by_row, (s,), (tm,)))(window)
    return pos, sched, n_used.reshape(1).astype(jnp.int32), tok.reshape(n_tiles, 1, tm)


def _row_copy(src_ref, src_row, dst_ref, dst_row, sem):
    src = src_ref.at[pl.ds(pl.multiple_of(src_row * SLAB, SLAB), SLAB)]
    dst = dst_ref.at[pl.ds(pl.multiple_of(dst_row * SLAB, SLAB), SLAB)]
    return pltpu.make_async_copy(src, dst, sem)


SCHED_EXPERT, SCHED_FIRST, SCHED_SLOT, SCHED_NEXT = range(4)


def _expert_kernel(layer, sched_ref, nu_ref, tok_ref, tokn_ref, xn_hbm, wgu_hbm, bgu_ref, wd_hbm, bd_ref,
                   ys_ref, x_a, x_b, wgu_f32, wd_f32, wgu_bf, wd_bf, act, gsem, sem):
    i = pl.program_id(0)
    last = pl.num_programs(0) - 1
    tm = act.shape[0]
    F = wd_bf.shape[0]
    e = sched_ref[SCHED_EXPERT, i]
    slot = sched_ref[SCHED_SLOT, i]
    nxt = sched_ref[SCHED_NEXT, i]
    active = i < nu_ref[0]

    def weight_copies(expert, s):
        return (pltpu.make_async_copy(wgu_hbm.at[layer, expert], wgu_f32.at[s], sem.at[0, s]),
                pltpu.make_async_copy(wd_hbm.at[layer, expert], wd_f32.at[s], sem.at[1, s]))

    @pl.when(jnp.logical_and(active, sched_ref[SCHED_FIRST, i] == 1))
    def _():
        @pl.when(i == 0)
        def _():
            for cp in weight_copies(e, slot):
                cp.start()

        @pl.when(nxt >= 0)
        def _():
            for cp in weight_copies(nxt, 1 - slot):
                cp.start()

        for cp in weight_copies(e, slot):
            cp.wait()
        wgu_bf[...] = wgu_f32[slot].astype(BF16)
        wd_bf[...] = wd_f32[slot].astype(BF16)

    def gather_loop(rows_ref, dst, s, wait):
        def body(jj, carry):
            for par in range(2):
                j = 2 * jj + par
                cp = _row_copy(xn_hbm, rows_ref[0, 0, j], dst, j, gsem.at[s])
                cp.wait() if wait else cp.start(priority=par)
            return carry

        lax.fori_loop(0, tm // 2, body, 0, unroll=4)

    @pl.when(i == 0)
    def _():
        gather_loop(tok_ref, x_a, 0, False)

    def step(cur, cur_s, nxt_buf, nxt_s):
        gather_loop(tok_ref, cur, cur_s, True)

        @pl.when(active)
        def _():
            for j in range(tm):
                _row_copy(xn_hbm, tokn_ref[0, 0, j], nxt_buf, j, gsem.at[nxt_s]).start(priority=j % 2)
            x = _slab_load(cur, tm).astype(BF16)
            fc = EXPERT_FC
            for c in range(F // fc):
                gate = _dot(x, wgu_bf[:, c * fc:(c + 1) * fc]) + bgu_ref[0, :, c * fc:(c + 1) * fc]
                up = _dot(x, wgu_bf[:, F + c * fc:F + (c + 1) * fc]) + bgu_ref[0, :, F + c * fc:F + (c + 1) * fc]
                gate = jnp.minimum(gate, SWIGLU_LIMIT)
                up = jnp.clip(up, -SWIGLU_LIMIT, SWIGLU_LIMIT)
                a = (up + 1.0) * gate * jax.nn.sigmoid(SWIGLU_ALPHA * gate)
                act[:, c * fc:(c + 1) * fc] = a.astype(BF16)
            _slab_store(ys_ref, _dot(act[...], wd_bf[...]) + bd_ref[0])

        @pl.when(jnp.logical_not(active))
        def _():
            gather_loop(tokn_ref, nxt_buf, nxt_s, False)
            ys_ref[...] = jnp.zeros(ys_ref.shape, F32)

        @pl.when(i == last)
        def _():
            gather_loop(tokn_ref, nxt_buf, nxt_s, True)

    @pl.when(i % 2 == 0)
    def _():
        step(x_a, 0, x_b, 1)

    @pl.when(i % 2 == 1)
    def _():
        step(x_b, 1, x_a, 0)


def _experts(xn, tok, sched, n_used, layer, w_gu, b_gu, w_down, b_down):
    n_tiles, _, tm = tok.shape
    P = n_tiles * tm
    L, E, D, F2 = w_gu.shape
    F = F2 // 2
    bsel = lambda i, sc, nu: (layer, sc[SCHED_EXPERT, i], 0, 0)
    return pl.pallas_call(
        functools.partial(_expert_kernel, layer),
        out_shape=jax.ShapeDtypeStruct((P * SLAB, LANES), F32),
        grid_spec=pltpu.PrefetchScalarGridSpec(
            num_scalar_prefetch=2,
            grid=(n_tiles,),
            in_specs=[
                pl.BlockSpec((1, 1, tm), lambda i, sc, nu: (i, 0, 0), memory_space=pltpu.SMEM),
                pl.BlockSpec((1, 1, tm), lambda i, sc, nu: (jnp.minimum(i + 1, n_tiles - 1), 0, 0),
                             memory_space=pltpu.SMEM),
                pl.BlockSpec(memory_space=pl.ANY),
                pl.BlockSpec(memory_space=pl.ANY),
                pl.BlockSpec((None, 1, 1, F2), bsel),
                pl.BlockSpec(memory_space=pl.ANY),
                pl.BlockSpec((None, 1, 1, D), bsel),
            ],
            out_specs=pl.BlockSpec((tm * SLAB, LANES), lambda i, sc, nu: (i, 0)),
            scratch_shapes=[
                pltpu.VMEM((tm * SLAB, LANES), F32),
                pltpu.VMEM((tm * SLAB, LANES), F32),
                pltpu.VMEM((2, D, F2), F32),
                pltpu.VMEM((2, F, D), F32),
                pltpu.VMEM((D, F2), BF16),
                pltpu.VMEM((F, D), BF16),
                pltpu.VMEM((tm, F), BF16),
                pltpu.SemaphoreType.DMA((2,)),
                pltpu.SemaphoreType.DMA((2, 2)),
            ],
        ),
        compiler_params=_params("arbitrary"),
        name="moe_experts",
    )(sched, n_used, tok, tok, xn, w_gu, b_gu.reshape(L, E, 1, F2), w_down, b_down.reshape(L, E, 1, D))


def _combine_kernel(pos_ref, posn_ref, gt_ref, x_ref, ys_ref, p_ref, g_ref, gw_ref, pw_ref, o_ref,
                    buf_a, buf_b, sem):
    i = pl.program_id(0)
    last = pl.num_programs(0) - 1
    tm = x_ref.shape[0]

    def gather_loop(rows_ref, dst, s, wait):
        def body(j, carry):
            for k in range(TOP_K):
                cp = _row_copy(ys_ref, rows_ref[k, j], dst.at[k], j, sem.at[s])
                cp.wait() if wait else cp.start(priority=k % 2)
            return carry

        lax.fori_loop(0, tm, body, 0, unroll=8)

    @pl.when(i == 0)
    def _():
        gather_loop(pos_ref, buf_a, 0, False)

    def step(cur, cur_s, nxt, nxt_s):
        gather_loop(pos_ref, cur, cur_s, True)
        for j in range(tm):
            for k in range(TOP_K):
                _row_copy(ys_ref, posn_ref[k, j], nxt.at[k], j, sem.at[nxt_s]).start(priority=k % 2)
        proj = _dot(p_ref[...].astype(BF16), pw_ref[...])
        gt = gt_ref[...]
        y = x_ref[...]
        for k in range(TOP_K):
            y = y + _slab_load(cur.at[k], tm) * gt[:, k:k + 1]
        h = _rms(y, g_ref[...])
        gate = jax.nn.sigmoid(_dot(h.astype(BF16), gw_ref[...]))
        o_ref[...] = y + gate * proj

        @pl.when(i == last)
        def _():
            gather_loop(posn_ref, nxt, nxt_s, True)

    @pl.when(i % 2 == 0)
    def _():
        step(buf_a, 0, buf_b, 1)

    @pl.when(i % 2 == 1)
    def _():
        step(buf_b, 1, buf_a, 0)


def _combine_ple(x, ys, pos, gates_t, layer, p, g, gate_w, proj_w):
    T, D = x.shape
    PD = p.shape[1]
    tm = COMBINE_TM
    n = T // tm
    row = lambda i: (i, 0)
    const = lambda i: (0, 0)
    return pl.pallas_call(
        _combine_kernel,
        out_shape=jax.ShapeDtypeStruct((T, D), F32),
        grid=(n,),
        in_specs=[
            pl.BlockSpec((TOP_K, tm), lambda i: (0, i), memory_space=pltpu.SMEM),
            pl.BlockSpec((TOP_K, tm), lambda i: (0, jnp.minimum(i + 1, n - 1)), memory_space=pltpu.SMEM),
            pl.BlockSpec((tm, TOP_K), row),
            pl.BlockSpec((tm, D), row),
            pl.BlockSpec(memory_space=pl.ANY),
            pl.BlockSpec((tm, PD), lambda i: (layer * n + i, 0)),
            pl.BlockSpec((1, D), const),
            pl.BlockSpec((D, D), const),
            pl.BlockSpec((PD, D), const),
        ],
        out_specs=pl.BlockSpec((tm, D), row),
        scratch_shapes=[pltpu.VMEM((TOP_K, tm * SLAB, LANES), F32),
                        pltpu.VMEM((TOP_K, tm * SLAB, LANES), F32),
                        pltpu.SemaphoreType.DMA((2,))],
        compiler_params=_params("arbitrary"),
        name="moe_combine_ple",
    )(pos, pos, gates_t.T, x, ys, p, g.reshape(1, D), gate_w.astype(BF16), proj_w.astype(BF16))


def _moe_ple(x, layer, p, norm_ffn, router_w, router_b, w_gu, b_gu, w_down, b_down, ple_norm, gate_w,
             proj_w):
    T, D = x.shape
    n_rows = T * TOP_K + N_EXPERTS * EXPERT_TM
    xn, top_i, gates_t, rank, cnt = _router(x, norm_ffn, router_w, router_b)
    pos, sched, n_used, tok = _routing_tables(top_i, rank, cnt[:, 0], EXPERT_TM, n_rows // EXPERT_TM)
    ys = _experts(xn, tok, sched, n_used, layer, w_gu, b_gu, w_down, b_down)
    return _combine_ple(x, ys, pos, gates_t, layer, p, ple_norm, gate_w, proj_w)


def _trig_kernel(pos_ref, f_ref, cos_ref, sin_ref):
    ang = pos_ref[...] * f_ref[...]
    cos_ref[...] = jnp.cos(ang)
    sin_ref[...] = jnp.sin(ang)


def _rope_tables(positions):
    T = positions.size
    half = ROPE_DIM // 2
    per_row = 128 // half
    inv_freq = ROPE_THETA ** (-jnp.arange(half, dtype=F32) / half)
    pos_rep = jnp.repeat(positions.reshape(T // per_row, per_row).astype(F32), half, axis=1)
    rows = T // per_row
    tr = TRIG_ROWS
    spec = pl.BlockSpec((tr, 128), lambda i: (i, 0))
    cos, sin = pl.pallas_call(
        _trig_kernel,
        out_shape=(jax.ShapeDtypeStruct((rows, 128), F32),) * 2,
        grid=(rows // tr,),
        in_specs=[spec, pl.BlockSpec((1, 128), lambda i: (0, 0))],
        out_specs=(spec, spec),
        compiler_params=_params("arbitrary"),
        name="rope_trig",
    )(pos_rep, jnp.tile(inv_freq, per_row).reshape(1, 128))
    cos = cos.reshape(T, half)
    sin = sin.reshape(T, half)
    z = jnp.zeros_like(cos)
    c_tab = jnp.concatenate([cos, cos, z, z], axis=1)
    s_lo = jnp.concatenate([-sin, z, z, z], axis=1)
    s_hi = jnp.concatenate([z, sin, z, z], axis=1)
    return c_tab, s_lo, s_hi


def _rope(blk, c_tab, s_lo, s_hi):
    half = ROPE_DIM // 2
    return (blk * c_tab + pltpu.roll(blk, 128 - half, 1) * s_lo + pltpu.roll(blk, half, 1) * s_hi)


def _mla_pre_kernel(x_ref, g_ref, win_ref, qn_ref, wq_ref, kvn_ref, wkv_ref, gq_ref, gk_ref,
                    c_ref, sl_ref, sh_ref, q_ref, k_ref, v_ref):
    H = q_ref.shape[1]
    h = _rms(x_ref[...], g_ref[...])
    lat = _dot(h.astype(BF16), win_ref[...])
    ql = _rms(lat[:, :Q_LORA], qn_ref[...])
    kvl = _rms(lat[:, Q_LORA:Q_LORA + KV_LORA], kvn_ref[...])
    kr = lat[:, Q_LORA + KV_LORA:]
    q = _dot(ql.astype(BF16), wq_ref[...])
    kv = _dot(kvl.astype(BF16), wkv_ref[...])
    c_tab, s_lo, s_hi = c_ref[...], sl_ref[...], sh_ref[...]
    gq, gk = gq_ref[...], gk_ref[...]
    kr_sq = kr * kr
    kr_rot = _rope(kr * gk[:, NOPE_DIM:], c_tab, s_lo, s_hi)
    qscale = QK_DIM ** -0.5 * LOG2_E
    ones = jnp.ones((NOPE_DIM, NOPE_DIM), BF16)
    for hd in range(H):
        q0 = q[:, hd * HEAD_PAD:hd * HEAD_PAD + NOPE_DIM]
        q1 = q[:, hd * HEAD_PAD + NOPE_DIM:(hd + 1) * HEAD_PAD]
        ss = _dot((q0 * q0 + q1 * q1).astype(BF16), ones)
        inv = lax.rsqrt(ss / QK_DIM + EPS) * qscale
        q_ref[0, hd, :, :NOPE_DIM] = (q0 * inv * gq[:, :NOPE_DIM]).astype(BF16)
        q_ref[0, hd, :, NOPE_DIM:] = (_rope(q1 * gq[:, NOPE_DIM:], c_tab, s_lo, s_hi) * inv).astype(BF16)
        k0 = kv[:, hd * 2 * NOPE_DIM:hd * 2 * NOPE_DIM + NOPE_DIM]
        ssk = _dot((k0 * k0 + kr_sq).astype(BF16), ones)
        invk = lax.rsqrt(ssk / QK_DIM + EPS)
        k_ref[0, hd, :, :NOPE_DIM] = (k0 * invk * gk[:, :NOPE_DIM]).astype(BF16)
        k_ref[0, hd, :, NOPE_DIM:] = (kr_rot * invk).astype(BF16)
        v_ref[0, hd] = kv[:, hd * 2 * NOPE_DIM + NOPE_DIM:(hd + 1) * 2 * NOPE_DIM].T.astype(BF16)


def _mla_pre(x, positions, g, w_in, q_norm, w_qb, kv_norm, w_kvb, qh_norm, kh_norm):
    B, S, D = x.shape
    H = MLA_HEADS
    tm = MLA_TM
    n_lat = Q_LORA + KV_LORA
    w_in_p = jnp.concatenate([w_in, jnp.zeros((D, 128 - ROPE_DIM), F32)], axis=1).astype(BF16)
    wq = w_qb.reshape(Q_LORA, H, QK_DIM)
    wq_p = jnp.concatenate([wq, jnp.zeros((Q_LORA, H, HEAD_PAD - QK_DIM), F32)], axis=2)
    wq_p = wq_p.reshape(Q_LORA, H * HEAD_PAD).astype(BF16)
    pad_gain = lambda n: jnp.concatenate([n, jnp.zeros((HEAD_PAD - QK_DIM,), F32)]).reshape(1, HEAD_PAD)
    c_tab, s_lo, s_hi = _rope_tables(positions)
    row = lambda b, s: (b * (S // tm) + s, 0)
    const = lambda b, s: (0, 0)
    head_spec = lambda w: pl.BlockSpec((1, H, tm, w), lambda b, s: (b, 0, s, 0))
    tab_spec = pl.BlockSpec((tm, 128), row)
    return pl.pallas_call(
        _mla_pre_kernel,
        out_shape=(
            jax.ShapeDtypeStruct((B, H, S, HEAD_PAD), BF16),
            jax.ShapeDtypeStruct((B, H, S, HEAD_PAD), BF16),
            jax.ShapeDtypeStruct((B, H, V_DIM, S), BF16),
        ),
        grid=(B, S // tm),
        in_specs=[
            pl.BlockSpec((tm, D), row),
            pl.BlockSpec((1, D), const),
            pl.BlockSpec((D, n_lat + 128), const),
            pl.BlockSpec((1, Q_LORA), const),
            pl.BlockSpec((Q_LORA, H * HEAD_PAD), const),
            pl.BlockSpec((1, KV_LORA), const),
            pl.BlockSpec((KV_LORA, H * 2 * NOPE_DIM), const),
            pl.BlockSpec((1, HEAD_PAD), const),
            pl.BlockSpec((1, HEAD_PAD), const),
            tab_spec, tab_spec, tab_spec,
        ],
        out_specs=(head_spec(HEAD_PAD), head_spec(HEAD_PAD),
                   pl.BlockSpec((1, H, V_DIM, tm), lambda b, s: (b, 0, 0, s))),
        compiler_params=_params("arbitrary", "arbitrary"),
        name="mla_qkv",
    )(x.reshape(B * S, D), g.reshape(1, D), w_in_p, q_norm.reshape(1, Q_LORA), wq_p,
      kv_norm.reshape(1, KV_LORA), w_kvb.astype(BF16), pad_gain(qh_norm), pad_gain(kh_norm),
      c_tab, s_lo, s_hi)


def _attn_kernel(q_ref, k_ref, vt_ref, o_ref, m_scr, l_scr, acc_scr, st_a, st_b):
    qi = pl.program_id(2)
    tq = q_ref.shape[2]
    q = q_ref[0, 0]
    m_scr[...] = jnp.full(m_scr.shape, -jnp.inf, F32)
    l_scr[...] = jnp.zeros(l_scr.shape, F32)
    acc_scr[...] = jnp.zeros(acc_scr.shape, F32)

    def scores(ki, st_ref):
        start = pl.multiple_of(ki * tq, tq)
        k = k_ref[0, 0, pl.ds(start, tq), :]
        st_ref[...] = lax.dot_general(k, q, (((1,), (1,)), ((), ())), preferred_element_type=F32)

    def fold(ki, st_ref, diagonal):
        start = pl.multiple_of(ki * tq, tq)
        vt = vt_ref[0, 0, :, pl.ds(start, tq)]
        st = st_ref[...]
        if diagonal:
            kc = lax.broadcasted_iota(jnp.int32, (tq, tq), 0) // CHUNK
            qc = lax.broadcasted_iota(jnp.int32, (tq, tq), 1) // CHUNK
            st = jnp.where(kc <= qc, st, -jnp.inf)
        m_prev = m_scr[...]
        m_new = jnp.maximum(m_prev, jnp.max(st, axis=0, keepdims=True))
        alpha = jnp.exp2(m_prev - m_new)
        p = jnp.exp2(st - m_new)
        l_scr[...] = alpha * l_scr[...] + jnp.sum(p, axis=0, keepdims=True)
        acc_scr[...] = alpha * acc_scr[...] + _dot(vt, p.astype(BF16))
        m_scr[...] = m_new

    scores(0, st_a)

    def pair(j, carry):
        scores(2 * j + 1, st_b)
        fold(2 * j, st_a, False)
        scores(2 * j + 2, st_a)
        fold(2 * j + 1, st_b, False)
        return carry

    lax.fori_loop(0, qi // 2, pair, 0)

    @pl.when(qi % 2 == 1)
    def _():
        scores(qi, st_b)
        fold(qi - 1, st_a, False)
        fold(qi, st_b, True)

    @pl.when(qi % 2 == 0)
    def _():
        fold(qi, st_a, True)

    o_ref[0] =(acc_scr[...] / l_scr[...]).T.astype(o_ref.dtype)


def _attention(q, k, vt):
    B, H, S, _ = q.shape
    tq = ATTN_TQ
    return pl.pallas_call(
        _attn_kernel,
        out_shape=jax.ShapeDtypeStruct((B, S, H * V_DIM), BF16),
        grid=(B, H, S // tq),
        in_specs=[
            pl.BlockSpec((1, 1, tq, HEAD_PAD), lambda b, h, i: (b, h, i, 0)),
            pl.BlockSpec((1, 1, S, HEAD_PAD), lambda b, h, i: (b, h, 0, 0)),
            pl.BlockSpec((1, 1, V_DIM, S), lambda b, h, i: (b, h, 0, 0)),
        ],
        out_specs=pl.BlockSpec((1, tq, V_DIM), lambda b, h, i: (b, i, h)),
        scratch_shapes=[
            pltpu.VMEM((1, tq), F32),
            pltpu.VMEM((1, tq), F32),
            pltpu.VMEM((V_DIM, tq), F32),
            pltpu.VMEM((tq, tq), F32),
            pltpu.VMEM((tq, tq), F32),
        ],
        compiler_params=_params("arbitrary", "arbitrary", "arbitrary"),
        name="mla_attention",
    )(q, k, vt)


def _oproj_kernel(x_ref, o_ref, w_ref, out_ref):
    out_ref[...] = x_ref[...] + _dot(o_ref[...], w_ref[...])


def _oproj(x, o, w_o):
    T, D = x.shape
    tm = OPROJ_TM
    row = lambda i: (i, 0)
    return pl.pallas_call(
        _oproj_kernel,
        out_shape=jax.ShapeDtypeStruct((T, D), F32),
        grid=(T // tm,),
        in_specs=[
            pl.BlockSpec((tm, D), row),
            pl.BlockSpec((tm, o.shape[1]), row),
            pl.BlockSpec(w_o.shape, lambda i: (0, 0)),
        ],
        out_specs=pl.BlockSpec((tm, D), row),
        compiler_params=_params("arbitrary"),
        name="mla_oproj",
    )(x, o, w_o.astype(BF16))


def _mla_layer(x, positions, g, w_in, q_norm, w_qb, kv_norm, w_kvb, qh_norm, kh_norm, w_o):
    B, S, D = x.shape
    q, k, v = _mla_pre(x, positions, g, w_in, q_norm, w_qb, kv_norm, w_kvb, qh_norm, kh_norm)
    o = _attention(q, k, v)
    return _oproj(x.reshape(B * S, D), o.reshape(B * S, -1), w_o).reshape(B, S, D)


def kernel(x, p, positions, norm_mix, pool_w, pool_scale, mla_w_in, mla_q_norm, mla_w_qb, mla_kv_norm, mla_w_kvb, mla_qh_norm, mla_kh_norm, mla_w_o, norm_ffn, router_w, router_b, w_gate_up, b_gate_up, w_down, b_down, ple_norm, ple_gate_w, ple_proj_w):
    B, S, D = x.shape
    depth = norm_mix.shape[0]
    p_rows = p.reshape(depth * B * S, -1)
    for i in range(depth):
        j = i // 2
        if i % 2 == 0:
            x = _pool_layer(x, norm_mix[i], pool_w[j], pool_scale[j])
        else:
            x = _mla_layer(x, positions, norm_mix[i], mla_w_in[j], mla_q_norm[j], mla_w_qb[j],
                           mla_kv_norm[j], mla_w_kvb[j], mla_qh_norm[j], mla_kh_norm[j], mla_w_o[j])
        x = _moe_ple(x.reshape(B * S, D), i, p_rows, norm_ffn[i], router_w[i], router_b[i],
                     w_gate_up, b_gate_up, w_down, b_down, ple_norm[i], ple_gate_w[i],
                     ple_proj_w[i]).reshape(B, S, D)
    return x
```

```python
import functools

import jax
import jax.numpy as jnp
from jax import lax
from jax.experimental import pallas as pl
from jax.experimental.pallas import tpu as pltpu

F32 = jnp.float32
BF16 = jnp.bfloat16

EPS = 1e-6
POOL_WINDOWS = (2, 4, 8, 16)
POOL_HALO = 16
CHUNK = 64
MLA_HEADS = 8
Q_LORA = 384
KV_LORA = 256
NOPE_DIM = 128
ROPE_DIM = 64
V_DIM = 128
QK_DIM = NOPE_DIM + ROPE_DIM
HEAD_PAD = 256
ROPE_THETA = 10000.0
LOG2_E = 1.4426950408889634
N_EXPERTS = 32
TOP_K = 4
SWIGLU_LIMIT = 7.0
SWIGLU_ALPHA = 1.702

V7X_VMEM_LIMIT = 56 * 1024 * 1024

POOL_TS = 512
ROUTER_TM = 512
EXPERT_TM = 512
EXPERT_FC = 512
COMBINE_TM = 256
MLA_TM = 512
ATTN_TQ = 512
OPROJ_TM = 512
TRIG_ROWS = 512


def _params(*sem):
    return pltpu.CompilerParams(dimension_semantics=sem, vmem_limit_bytes=V7X_VMEM_LIMIT)


def _rms(x, g):
    return x * lax.rsqrt(jnp.mean(x * x, axis=-1, keepdims=True) + EPS) * g


def _dot(a, b):
    return jnp.dot(a, b, preferred_element_type=F32)


SLAB = 8
LANES = 128


def _slab_load(ref, n):
    return jnp.concatenate([ref[pl.ds(s, n, stride=SLAB), :] for s in range(SLAB)], axis=-1)


def _slab_store(ref, x):
    n = x.shape[0]
    for s in range(SLAB):
        ref[pl.ds(s, n, stride=SLAB), :] = x[:, s * LANES:(s + 1) * LANES]


def _pool_kernel(x_ref, g_ref, w_ref, sc_ref, o_ref, hbuf):
    s = pl.program_id(1)
    ts = x_ref.shape[1]
    gd = w_ref.shape[1]
    x = x_ref[0]
    h = _rms(x, g_ref[...])

    @pl.when(s == 0)
    def _():
        hbuf[0:POOL_HALO, :] = jnp.zeros((POOL_HALO, x.shape[1]), F32)

    hbuf[POOL_HALO:, :] = h
    pos1 = s * ts + lax.broadcasted_iota(jnp.int32, (ts, 1), 0) + 1
    outs = []
    for g, win in enumerate(POOL_WINDOWS):
        lo, hi = g * gd, (g + 1) * gd
        hg = h[:, lo:hi]
        acc = hg
        for j in range(1, win):
            acc = acc + hbuf[POOL_HALO - j:POOL_HALO - j + ts, lo:hi]
        cnt = jnp.minimum(pos1, win).astype(F32)
        diff = acc / cnt - hg
        outs.append(_dot(diff.astype(BF16), w_ref[g]))
    mix = jnp.concatenate(outs, axis=-1) * sc_ref[...]
    o_ref[0] = x + mix
    hbuf[0:POOL_HALO, :] = hbuf[ts:ts + POOL_HALO, :]


def _pool_layer(x, g, w, scale):
    B, S, D = x.shape
    ts = POOL_TS
    G, gd, _ = w.shape
    return pl.pallas_call(
        _pool_kernel,
        out_shape=jax.ShapeDtypeStruct((B, S, D), F32),
        grid=(B, S // ts),
        in_specs=[
            pl.BlockSpec((1, ts, D), lambda b, s: (b, s, 0)),
            pl.BlockSpec((1, D), lambda b, s: (0, 0)),
            pl.BlockSpec((G, gd, gd), lambda b, s: (0, 0, 0)),
            pl.BlockSpec((1, D), lambda b, s: (0, 0)),
        ],
        out_specs=pl.BlockSpec((1, ts, D), lambda b, s: (b, s, 0)),
        scratch_shapes=[pltpu.VMEM((ts + POOL_HALO, D), F32)],
        compiler_params=_params("arbitrary", "arbitrary"),
        name="pool_mixer",
    )(x, g.reshape(1, D), w.astype(BF16), scale.reshape(1, D))


def _router_kernel(x_ref, g_ref, rwt_ref, rb_ref, xn_ref, ti_ref, gt_ref, rk_ref, cnt_ref, run):
    i = pl.program_id(0)
    tm = x_ref.shape[0]
    E = rwt_ref.shape[0]

    @pl.when(i == 0)
    def _():
        run[...] = jnp.zeros(run.shape, F32)

    xn = _rms(x_ref[...], g_ref[...])
    _slab_store(xn_ref, xn)
    logits =lax.dot_general(rwt_ref[...], xn, (((1,), (1,)), ((), ())),
                             precision=lax.Precision.HIGHEST,
                             preferred_element_type=F32) + rb_ref[...]
    eidx = lax.broadcasted_iota(jnp.int32, (E, tm), 0)
    vals, sels = [], []
    l = logits
    for k in range(TOP_K):
        m = jnp.max(l, axis=0, keepdims=True)
        idx = jnp.min(jnp.where(l == m, eidx, E), axis=0, keepdims=True)
        sel = eidx == idx
        ti_ref[k:k + 1, :] = idx
        vals.append(m)
        sels.append(sel)
        l = jnp.where(sel, -jnp.inf, l)
    ex = [jnp.exp(v - vals[0]) for v in vals]
    den = ex[0] + ex[1] + ex[2] + ex[3]
    for k in range(TOP_K):
        gt_ref[k:k + 1, :] = ex[k] / den
    member = jnp.zeros((E, tm), F32)
    for sel in sels:
        member = member + sel.astype(F32)
    r = lax.broadcasted_iota(jnp.int32, (tm, tm), 0)
    c = lax.broadcasted_iota(jnp.int32, (tm, tm), 1)
    before = jnp.where(r < c, 1.0, 0.0).astype(BF16)
    prior = _dot(member.astype(BF16), before) + run[:, 0:1]
    for k in range(TOP_K):
        rk = jnp.sum(jnp.where(sels[k], prior, 0.0), axis=0, keepdims=True)
        rk_ref[k:k + 1, :] = rk.astype(jnp.int32)
    run[...] = run[...] + jnp.sum(member, axis=1, keepdims=True)
    cnt_ref[...] = run[...]


def _router(x, g, rw, rb):
    T, D = x.shape
    E = rw.shape[1]
    tm = ROUTER_TM
    row = lambda i: (i, 0)
    col = lambda i: (0, i)
    const = lambda i: (0, 0)
    return pl.pallas_call(
        _router_kernel,
        out_shape=(
            jax.ShapeDtypeStruct((T * SLAB, LANES), F32),
            jax.ShapeDtypeStruct((TOP_K, T), jnp.int32),
            jax.ShapeDtypeStruct((TOP_K, T), F32),
            jax.ShapeDtypeStruct((TOP_K, T), jnp.int32),
            jax.ShapeDtypeStruct((E, 128), F32),
        ),
        grid=(T // tm,),
        in_specs=[
            pl.BlockSpec((tm, D), row),
            pl.BlockSpec((1, D), const),
            pl.BlockSpec((E, D), const),
            pl.BlockSpec((E, 1), const),
        ],
        out_specs=(
            pl.BlockSpec((tm * SLAB, LANES), row),
            pl.BlockSpec((TOP_K, tm), col),
            pl.BlockSpec((TOP_K, tm), col),
            pl.BlockSpec((TOP_K, tm), col),
            pl.BlockSpec((E, 128), const),
        ),
        scratch_shapes=[pltpu.VMEM((E, 128), F32)],
        compiler_params=_params("arbitrary"),
        name="moe_router",
    )(x, g.reshape(1, D), rw.T, rb.reshape(E, 1))


def _routing_tables(top_i, rank, counts, tm, n_tiles):
    counts = counts.astype(jnp.int32)
    padded = ((counts + tm - 1) // tm) * tm
    ends = jnp.cumsum(padded)
    offs = ends - padded
    e_ids = jnp.arange(N_EXPERTS, dtype=jnp.int32)[:, None, None]
    pos = rank + jnp.sum(jnp.where(top_i[None] == e_ids, offs[:, None, None], 0), axis=0)
    n_used = ends[-1] // tm
    tiles = jnp.arange(n_tiles, dtype=jnp.int32)
    te = jnp.sum((ends[None, :] <= (tiles * tm)[:, None]).astype(jnp.int32), axis=1)
    te = jnp.minimum(te, N_EXPERTS - 1)
    te = jnp.where(tiles < n_used, te, jnp.take(te, n_used - 1))
    first = jnp.logical_and(tiles < n_used, te != jnp.concatenate([jnp.full((1,), -1, jnp.int32), te[:-1]]))
    slot = (jnp.cumsum(first.astype(jnp.int32)) - 1) % 2
    nxt_tile = jnp.take(ends, te) // tm
    nxt = jnp.where(nxt_tile < n_used, jnp.take(te, jnp.minimum(nxt_tile, n_tiles - 1)), -1)
    sched = jnp.stack([te, first.astype(jnp.int32), slot, nxt]).astype(jnp.int32)
    T = top_i.shape[1]
    n_dummy = n_tiles * tm - TOP_K * T
    d = jnp.arange(n_dummy, dtype=jnp.int32)
    pad_ends = jnp.cumsum(padded - counts)
    d_exp = jnp.sum((pad_ends[None, :] <= d[:, None]).astype(jnp.int32), axis=1)
    in_pad = d_exp < N_EXPERTS
    d_exp = jnp.minimum(d_exp, N_EXPERTS - 1)
    pick = lambda v: jnp.sum(jnp.where(d_exp[:, None] == e_ids[:, 0, 0][None, :], v[None, :], 0), axis=1)
    pad_key = pick(offs + counts) + d - pick(pad_ends - (padded - counts))
    dummy_key = jnp.where(in_pad, pad_key, ends[-1] + d - pad_ends[-1])
    keys = jnp.concatenate([pos.reshape(-1), dummy_key])
    vals = jnp.concatenate([jnp.tile(jnp.arange(T, dtype=jnp.int32), TOP_K), jnp.zeros((n_dummy,), jnp.int32)])
    _, tok = lax.sort_key_val(keys, vals)
    return pos, sched, n_used.reshape(1).astype(jnp.int32), tok.reshape(n_tiles, 1, tm)


def _row_copy(src_ref, src_row, dst_ref, dst_row, sem):
    src = src_ref.at[pl.ds(pl.multiple_of(src_row * SLAB, SLAB), SLAB)]
    dst = dst_ref.at[pl.ds(pl.multiple_of(dst_row * SLAB, SLAB), SLAB)]
    return pltpu.make_async_copy(src, dst, sem)


SCHED_EXPERT, SCHED_FIRST, SCHED_SLOT, SCHED_NEXT = range(4)


GATHER_AHEAD = 2
GATHER_RING = GATHER_AHEAD + 1


def _expert_kernel(layer, sched_ref, nu_ref, tok_ref, tok1_ref, tok2_ref, xn_hbm, wgu_hbm, bgu_ref, wd_hbm,
                   bd_ref, ys_ref, x0, x1, x2, wgu_f32, wd_f32, wgu_bf, wd_bf, act, gsem, sem):
    i = pl.program_id(0)
    last = pl.num_programs(0) - 1
    tm = act.shape[0]
    F = wd_bf.shape[0]
    e = sched_ref[SCHED_EXPERT, i]
    slot = sched_ref[SCHED_SLOT, i]
    nxt = sched_ref[SCHED_NEXT, i]
    active = i < nu_ref[0]

    def weight_copies(expert, s):
        return (pltpu.make_async_copy(wgu_hbm.at[layer, expert], wgu_f32.at[s], sem.at[0, s]),
                pltpu.make_async_copy(wd_hbm.at[layer, expert], wd_f32.at[s], sem.at[1, s]))

    @pl.when(jnp.logical_and(active, sched_ref[SCHED_FIRST, i] == 1))
    def _():
        @pl.when(i == 0)
        def _():
            for cp in weight_copies(e, slot):
                cp.start()

        @pl.when(nxt >= 0)
        def _():
            for cp in weight_copies(nxt, 1 - slot):
                cp.start()

        for cp in weight_copies(e, slot):
            cp.wait()
        wgu_bf[...] = wgu_f32[slot].astype(BF16)
        wd_bf[...] = wd_f32[slot].astype(BF16)

    xbufs = (x0, x1, x2)
    assert len(xbufs) == GATHER_RING

    def gather_loop(rows_ref, r, wait):
        def body(jj, carry):
            for par in range(2):
                j = 2 * jj + par
                cp = _row_copy(xn_hbm, rows_ref[0, 0, j], xbufs[r], j, gsem.at[r])
                cp.wait() if wait else cp.start(priority=par)
            return carry

        lax.fori_loop(0, tm // 2, body, 0, unroll=4)

    @pl.when(i == 0)
    def _():
        gather_loop(tok_ref, 0, False)
        gather_loop(tok1_ref, 1, False)

    def step(cur):
        ahead = (cur + GATHER_AHEAD) % GATHER_RING
        gather_loop(tok_ref, cur, True)

        @pl.when(active)
        def _():
            for j in range(tm):
                _row_copy(xn_hbm, tok2_ref[0, 0, j], xbufs[ahead], j, gsem.at[ahead]).start(priority=j % 2)
            x = _slab_load(xbufs[cur], tm).astype(BF16)
            fc = EXPERT_FC
            for c in range(F // fc):
                gate = _dot(x, wgu_bf[:, c * fc:(c + 1) * fc]) + bgu_ref[0, :, c * fc:(c + 1) * fc]
                up = _dot(x, wgu_bf[:, F + c * fc:F + (c + 1) * fc]) + bgu_ref[0, :, F + c * fc:F + (c + 1) * fc]
                gate = jnp.minimum(gate, SWIGLU_LIMIT)
                up = jnp.clip(up, -SWIGLU_LIMIT, SWIGLU_LIMIT)
                a = (up + 1.0) * gate * jax.nn.sigmoid(SWIGLU_ALPHA * gate)
                act[:, c * fc:(c + 1) * fc] = a.astype(BF16)
            _slab_store(ys_ref, _dot(act[...], wd_bf[...]) + bd_ref[0])

        @pl.when(jnp.logical_not(active))
        def _():
            gather_loop(tok2_ref, ahead, False)
            ys_ref[...] = jnp.zeros(ys_ref.shape, F32)

        @pl.when(i == last)
        def _():
            gather_loop(tok_ref, (cur + 1) % GATHER_RING, True)
            gather_loop(tok_ref, ahead, True)

    for r in range(GATHER_RING):
        pl.when(i % GATHER_RING == r)(functools.partial(step, r))


def _experts(xn, tok, sched, n_used, layer, w_gu, b_gu, w_down, b_down):
    n_tiles, _, tm = tok.shape
    P = n_tiles * tm
    L, E, D, F2 = w_gu.shape
    F = F2 // 2
    bsel = lambda i, sc, nu: (layer, sc[SCHED_EXPERT, i], 0, 0)
    tok_spec = lambda ahead: pl.BlockSpec(
        (1, 1, tm), lambda i, sc, nu: (jnp.minimum(i + ahead, n_tiles - 1), 0, 0), memory_space=pltpu.SMEM)
    return pl.pallas_call(
        functools.partial(_expert_kernel, layer),
        out_shape=jax.ShapeDtypeStruct((P * SLAB, LANES), F32),
        grid_spec=pltpu.PrefetchScalarGridSpec(
            num_scalar_prefetch=2,
            grid=(n_tiles,),
            in_specs=[
                tok_spec(0), tok_spec(1), tok_spec(2),
                pl.BlockSpec(memory_space=pl.ANY),
                pl.BlockSpec(memory_space=pl.ANY),
                pl.BlockSpec((None, 1, 1, F2), bsel),
                pl.BlockSpec(memory_space=pl.ANY),
                pl.BlockSpec((None, 1, 1, D), bsel),
            ],
            out_specs=pl.BlockSpec((tm * SLAB, LANES), lambda i, sc, nu: (i, 0)),
            scratch_shapes=[
                pltpu.VMEM((tm * SLAB, LANES), F32),
                pltpu.VMEM((tm * SLAB, LANES), F32),
                pltpu.VMEM((tm * SLAB, LANES), F32),
                pltpu.VMEM((2, D, F2), F32),
                pltpu.VMEM((2, F, D), F32),
                pltpu.VMEM((D, F2), BF16),
                pltpu.VMEM((F, D), BF16),
                pltpu.VMEM((tm, F), BF16),
                pltpu.SemaphoreType.DMA((GATHER_RING,)),
                pltpu.SemaphoreType.DMA((2, 2)),
            ],
        ),
        compiler_params=_params("arbitrary"),
        name="moe_experts",
    )(sched, n_used, tok, tok, tok, xn, w_gu, b_gu.reshape(L, E, 1, F2), w_down, b_down.reshape(L, E, 1, D))


def _combine_kernel(pos_ref, posn_ref, gt_ref, x_ref, ys_ref, p_ref, g_ref, gw_ref, pw_ref, o_ref,
                    buf_a, buf_b, sem):
    i = pl.program_id(0)
    last = pl.num_programs(0) - 1
    tm = x_ref.shape[0]

    def gather_loop(rows_ref, dst, s, wait):
        def body(j, carry):
            for k in range(TOP_K):
                cp = _row_copy(ys_ref, rows_ref[k, j], dst.at[k], j, sem.at[s])
                cp.wait() if wait else cp.start(priority=k % 2)
            return carry

        lax.fori_loop(0, tm, body, 0, unroll=8)

    @pl.when(i == 0)
    def _():
        gather_loop(pos_ref, buf_a, 0, False)

    def step(cur, cur_s, nxt, nxt_s):
        gather_loop(pos_ref, cur, cur_s, True)
        for j in range(tm):
            for k in range(TOP_K):
                _row_copy(ys_ref, posn_ref[k, j], nxt.at[k], j, sem.at[nxt_s]).start(priority=k % 2)
        proj = _dot(p_ref[...].astype(BF16), pw_ref[...])
        gt = gt_ref[...]
        y = x_ref[...]
        for k in range(TOP_K):
            y = y + _slab_load(cur.at[k], tm) * gt[:, k:k + 1]
        h = _rms(y, g_ref[...])
        gate = jax.nn.sigmoid(_dot(h.astype(BF16), gw_ref[...]))
        o_ref[...] = y + gate * proj

        @pl.when(i == last)
        def _():
            gather_loop(posn_ref, nxt, nxt_s, True)

    @pl.when(i % 2 == 0)
    def _():
        step(buf_a, 0, buf_b, 1)

    @pl.when(i % 2 == 1)
    def _():
        step(buf_b, 1, buf_a, 0)


def _combine_ple(x, ys, pos, gates_t, layer, p, g, gate_w, proj_w):
    T, D = x.shape
    PD = p.shape[1]
    tm = COMBINE_TM
    n = T // tm
    row = lambda i: (i, 0)
    const = lambda i: (0, 0)
    return pl.pallas_call(
        _combine_kernel,
        out_shape=jax.ShapeDtypeStruct((T, D), F32),
        grid=(n,),
        in_specs=[
            pl.BlockSpec((TOP_K, tm), lambda i: (0, i), memory_space=pltpu.SMEM),
            pl.BlockSpec((TOP_K, tm), lambda i: (0, jnp.minimum(i + 1, n - 1)), memory_space=pltpu.SMEM),
            pl.BlockSpec((tm, TOP_K), row),
            pl.BlockSpec((tm, D), row),
            pl.BlockSpec(memory_space=pl.ANY),
            pl.BlockSpec((tm, PD), lambda i: (layer * n + i, 0)),
            pl.BlockSpec((1, D), const),
            pl.BlockSpec((D, D), const),
            pl.BlockSpec((PD, D), const),
        ],
        out_specs=pl.BlockSpec((tm, D), row),
        scratch_shapes=[pltpu.VMEM((TOP_K, tm * SLAB, LANES), F32),
                        pltpu.VMEM((TOP_K, tm * SLAB, LANES), F32),
                        pltpu.SemaphoreType.DMA((2,))],
        compiler_params=_params("arbitrary"),
        name="moe_combine_ple",
    )(pos, pos, gates_t.T, x, ys, p, g.reshape(1, D), gate_w.astype(BF16), proj_w.astype(BF16))


def _moe_ple(x, layer, p, norm_ffn, router_w, router_b, w_gu, b_gu, w_down, b_down, ple_norm, gate_w,
             proj_w):
    T, D = x.shape
    n_rows = T * TOP_K + N_EXPERTS * EXPERT_TM
    xn, top_i, gates_t, rank, cnt = _router(x, norm_ffn, router_w, router_b)
    pos, sched, n_used, tok = _routing_tables(top_i, rank, cnt[:, 0], EXPERT_TM, n_rows // EXPERT_TM)
    ys = _experts(xn, tok, sched, n_used, layer, w_gu, b_gu, w_down, b_down)
    return _combine_ple(x, ys, pos, gates_t, layer, p, ple_norm, gate_w, proj_w)


def _trig_kernel(pos_ref, f_ref, cos_ref, sin_ref):
    ang = pos_ref[...] * f_ref[...]
    cos_ref[...] = jnp.cos(ang)
    sin_ref[...] = jnp.sin(ang)


def _rope_tables(positions):
    T = positions.size
    half = ROPE_DIM // 2
    per_row = 128 // half
    inv_freq = ROPE_THETA ** (-jnp.arange(half, dtype=F32) / half)
    pos_rep = jnp.repeat(positions.reshape(T // per_row, per_row).astype(F32), half, axis=1)
    rows = T // per_row
    tr = TRIG_ROWS
    spec = pl.BlockSpec((tr, 128), lambda i: (i, 0))
    cos, sin = pl.pallas_call(
        _trig_kernel,
        out_shape=(jax.ShapeDtypeStruct((rows, 128), F32),) * 2,
        grid=(rows // tr,),
        in_specs=[spec, pl.BlockSpec((1, 128), lambda i: (0, 0))],
        out_specs=(spec, spec),
        compiler_params=_params("arbitrary"),
        name="rope_trig",
    )(pos_rep, jnp.tile(inv_freq, per_row).reshape(1, 128))
    cos = cos.reshape(T, half)
    sin = sin.reshape(T, half)
    z = jnp.zeros_like(cos)
    c_tab = jnp.concatenate([cos, cos, z, z], axis=1)
    s_lo = jnp.concatenate([-sin, z, z, z], axis=1)
    s_hi = jnp.concatenate([z, sin, z, z], axis=1)
    return c_tab, s_lo, s_hi


def _rope(blk, c_tab, s_lo, s_hi):
    half = ROPE_DIM // 2
    return (blk * c_tab + pltpu.roll(blk, 128 - half, 1) * s_lo + pltpu.roll(blk, half, 1) * s_hi)


def _mla_pre_kernel(x_ref, g_ref, win_ref, qn_ref, wq_ref, kvn_ref, wkv_ref, gq_ref, gk_ref,
                    c_ref, sl_ref, sh_ref, q_ref, k_ref, v_ref):
    H = q_ref.shape[1]
    h = _rms(x_ref[...], g_ref[...])
    lat = _dot(h.astype(BF16), win_ref[...])
    ql = _rms(lat[:, :Q_LORA], qn_ref[...])
    kvl = _rms(lat[:, Q_LORA:Q_LORA + KV_LORA], kvn_ref[...])
    kr = lat[:, Q_LORA + KV_LORA:]
    q = _dot(ql.astype(BF16), wq_ref[...])
    kv = _dot(kvl.astype(BF16), wkv_ref[...])
    c_tab, s_lo, s_hi = c_ref[...], sl_ref[...], sh_ref[...]
    gq, gk = gq_ref[...], gk_ref[...]
    kr_sq = kr * kr
    kr_rot = _rope(kr * gk[:, NOPE_DIM:], c_tab, s_lo, s_hi)
    qscale = QK_DIM ** -0.5 * LOG2_E
    ones = jnp.ones((NOPE_DIM, NOPE_DIM), BF16)
    for hd in range(H):
        q0 = q[:, hd * HEAD_PAD:hd * HEAD_PAD + NOPE_DIM]
        q1 = q[:, hd * HEAD_PAD + NOPE_DIM:(hd + 1) * HEAD_PAD]
        ss = _dot((q0 * q0 + q1 * q1).astype(BF16), ones)
        inv = lax.rsqrt(ss / QK_DIM + EPS) * qscale
        q_ref[0, hd, :, :NOPE_DIM] = (q0 * inv * gq[:, :NOPE_DIM]).astype(BF16)
        q_ref[0, hd, :, NOPE_DIM:] = (_rope(q1 * gq[:, NOPE_DIM:], c_tab, s_lo, s_hi) * inv).astype(BF16)
        k0 = kv[:, hd * 2 * NOPE_DIM:hd * 2 * NOPE_DIM + NOPE_DIM]
        ssk = _dot((k0 * k0 + kr_sq).astype(BF16), ones)
        invk = lax.rsqrt(ssk / QK_DIM + EPS)
        k_ref[0, hd, :, :NOPE_DIM] = (k0 * invk * gk[:, :NOPE_DIM]).astype(BF16)
        k_ref[0, hd, :, NOPE_DIM:] = (kr_rot * invk).astype(BF16)
        v_ref[0, hd] = kv[:, hd * 2 * NOPE_DIM + NOPE_DIM:(hd + 1) * 2 * NOPE_DIM].T.astype(BF16)


def _mla_pre(x, positions, g, w_in, q_norm, w_qb, kv_norm, w_kvb, qh_norm, kh_norm):
    B, S, D = x.shape
    H = MLA_HEADS
    tm = MLA_TM
    n_lat = Q_LORA + KV_LORA
    w_in_p = jnp.concatenate([w_in, jnp.zeros((D, 128 - ROPE_DIM), F32)], axis=1).astype(BF16)
    wq = w_qb.reshape(Q_LORA, H, QK_DIM)
    wq_p = jnp.concatenate([wq, jnp.zeros((Q_LORA, H, HEAD_PAD - QK_DIM), F32)], axis=2)
    wq_p = wq_p.reshape(Q_LORA, H * HEAD_PAD).astype(BF16)
    pad_gain = lambda n: jnp.concatenate([n, jnp.zeros((HEAD_PAD - QK_DIM,), F32)]).reshape(1, HEAD_PAD)
    c_tab, s_lo, s_hi = _rope_tables(positions)
    row = lambda b, s: (b * (S // tm) + s, 0)
    const = lambda b, s: (0, 0)
    head_spec = lambda w: pl.BlockSpec((1, H, tm, w), lambda b, s: (b, 0, s, 0))
    tab_spec = pl.BlockSpec((tm, 128), row)
    return pl.pallas_call(
        _mla_pre_kernel,
        out_shape=(
            jax.ShapeDtypeStruct((B, H, S, HEAD_PAD), BF16),
            jax.ShapeDtypeStruct((B, H, S, HEAD_PAD), BF16),
            jax.ShapeDtypeStruct((B, H, V_DIM, S), BF16),
        ),
        grid=(B, S // tm),
        in_specs=[
            pl.BlockSpec((tm, D), row),
            pl.BlockSpec((1, D), const),
            pl.BlockSpec((D, n_lat + 128), const),
            pl.BlockSpec((1, Q_LORA), const),
            pl.BlockSpec((Q_LORA, H * HEAD_PAD), const),
            pl.BlockSpec((1, KV_LORA), const),
            pl.BlockSpec((KV_LORA, H * 2 * NOPE_DIM), const),
            pl.BlockSpec((1, HEAD_PAD), const),
            pl.BlockSpec((1, HEAD_PAD), const),
            tab_spec, tab_spec, tab_spec,
        ],
        out_specs=(head_spec(HEAD_PAD), head_spec(HEAD_PAD),
                   pl.BlockSpec((1, H, V_DIM, tm), lambda b, s: (b, 0, 0, s))),
        compiler_params=_params("arbitrary", "arbitrary"),
        name="mla_qkv",
    )(x.reshape(B * S, D), g.reshape(1, D), w_in_p, q_norm.reshape(1, Q_LORA), wq_p,
      kv_norm.reshape(1, KV_LORA), w_kvb.astype(BF16), pad_gain(qh_norm), pad_gain(kh_norm),
      c_tab, s_lo, s_hi)


def _attn_kernel(q_ref, k_ref, vt_ref, o_ref, m_scr, l_scr, acc_scr, st_a, st_b):
    qi = pl.program_id(2)
    tq = q_ref.shape[2]
    q = q_ref[0, 0]
    m_scr[...] = jnp.full(m_scr.shape, -jnp.inf, F32)
    l_scr[...] = jnp.zeros(l_scr.shape, F32)
    acc_scr[...] = jnp.zeros(acc_scr.shape, F32)

    def scores(ki, st_ref):
        start = pl.multiple_of(ki * tq, tq)
        k = k_ref[0, 0, pl.ds(start, tq), :]
        st_ref[...] = lax.dot_general(k, q, (((1,), (1,)), ((), ())), preferred_element_type=F32)

    def fold(ki, st_ref, diagonal):
        start = pl.multiple_of(ki * tq, tq)
        vt = vt_ref[0, 0, :, pl.ds(start, tq)]
        st = st_ref[...]
        if diagonal:
            kc = lax.broadcasted_iota(jnp.int32, (tq, tq), 0) // CHUNK
            qc = lax.broadcasted_iota(jnp.int32, (tq, tq), 1) // CHUNK
            st = jnp.where(kc <= qc, st, -jnp.inf)
        m_prev = m_scr[...]
        m_new = jnp.maximum(m_prev, jnp.max(st, axis=0, keepdims=True))
        alpha = jnp.exp2(m_prev - m_new)
        p = jnp.exp2(st - m_new)
        l_scr[...] = alpha * l_scr[...] + jnp.sum(p, axis=0, keepdims=True)
        acc_scr[...] = alpha * acc_scr[...] + _dot(vt, p.astype(BF16))
        m_scr[...] = m_new

    scores(0, st_a)

    def pair(j, carry):
        scores(2 * j + 1, st_b)
        fold(2 * j, st_a, False)
        scores(2 * j + 2, st_a)
        fold(2 * j + 1, st_b, False)
        return carry

    lax.fori_loop(0, qi // 2, pair, 0)

    @pl.when(qi % 2 == 1)
    def _():
        scores(qi, st_b)
        fold(qi - 1, st_a, False)
        fold(qi, st_b, True)

    @pl.when(qi % 2 == 0)
    def _():
        fold(qi, st_a, True)

    o_ref[0] =(acc_scr[...] / l_scr[...]).T.astype(o_ref.dtype)


def _attention(q, k, vt):
    B, H, S, _ = q.shape
    tq = ATTN_TQ
    return pl.pallas_call(
        _attn_kernel,
        out_shape=jax.ShapeDtypeStruct((B, S, H * V_DIM), BF16),
        grid=(B, H, S // tq),
        in_specs=[
            pl.BlockSpec((1, 1, tq, HEAD_PAD), lambda b, h, i: (b, h, i, 0)),
            pl.BlockSpec((1, 1, S, HEAD_PAD), lambda b, h, i: (b, h, 0, 0)),
            pl.BlockSpec((1, 1, V_DIM, S), lambda b, h, i: (b, h, 0, 0)),
        ],
        out_specs=pl.BlockSpec((1, tq, V_DIM), lambda b, h, i: (b, i, h)),
        scratch_shapes=[
            pltpu.VMEM((1, tq), F32),
            pltpu.VMEM((1, tq), F32),
            pltpu.VMEM((V_DIM, tq), F32),
            pltpu.VMEM((tq, tq), F32),
            pltpu.VMEM((tq, tq), F32),
        ],
        compiler_params=_params("arbitrary", "arbitrary", "arbitrary"),
        name="mla_attention",
    )(q, k, vt)


def _oproj_kernel(x_ref, o_ref, w_ref, out_ref):
    out_ref[...] = x_ref[...] + _dot(o_ref[...], w_ref[...])


def _oproj(x, o, w_o):
    T, D = x.shape
    tm = OPROJ_TM
    row = lambda i: (i, 0)
    return pl.pallas_call(
        _oproj_kernel,
        out_shape=jax.ShapeDtypeStruct((T, D), F32),
        grid=(T // tm,),
        in_specs=[
            pl.BlockSpec((tm, D), row),
            pl.BlockSpec((tm, o.shape[1]), row),
            pl.BlockSpec(w_o.shape, lambda i: (0, 0)),
        ],
        out_specs=pl.BlockSpec((tm, D), row),
        compiler_params=_params("arbitrary"),
        name="mla_oproj",
    )(x, o, w_o.astype(BF16))


def _mla_layer(x, positions, g, w_in, q_norm, w_qb, kv_norm, w_kvb, qh_norm, kh_norm, w_o):
    B, S, D = x.shape
    q, k, v = _mla_pre(x, positions, g, w_in, q_norm, w_qb, kv_norm, w_kvb, qh_norm, kh_norm)
    o = _attention(q, k, v)
    return _oproj(x.reshape(B * S, D), o.reshape(B * S, -1), w_o).reshape(B, S, D)


def kernel(x, p, positions, norm_mix, pool_w, pool_scale, mla_w_in, mla_q_norm, mla_w_qb, mla_kv_norm, mla_w_kvb, mla_qh_norm, mla_kh_norm, mla_w_o, norm_ffn, router_w, router_b, w_gate_up, b_gate_up, w_down, b_down, ple_norm, ple_gate_w, ple_proj_w):
    B, S, D = x.shape
    depth = norm_mix.shape[0]
    p_rows = p.reshape(depth * B * S, -1)
    for i in range(depth):
        j = i // 2
        if i % 2 == 0:
            x = _pool_layer(x, norm_mix[i], pool_w[j], pool_scale[j])
        else:
            x = _mla_layer(x, positions, norm_mix[i], mla_w_in[j], mla_q_norm[j], mla_w_qb[j],
                           mla_kv_norm[j], mla_w_kvb[j], mla_qh_norm[j], mla_kh_norm[j], mla_w_o[j])
        x = _moe_ple(x.reshape(B * S, D), i, p_rows, norm_ffn[i], router_w[i], router_b[i],
                     w_gate_up, b_gate_up, w_down, b_down, ple_norm[i], ple_gate_w[i],
                     ple_proj_w[i]).reshape(B, S, D)
    return x
```

```python
import functools

import jax
import jax.numpy as jnp
from jax import lax
from jax.experimental import pallas as pl
from jax.experimental.pallas import tpu as pltpu

F32 = jnp.float32
BF16 = jnp.bfloat16

EPS = 1e-6
POOL_WINDOWS = (2, 4, 8, 16)
POOL_HALO = 16
CHUNK = 64
MLA_HEADS = 8
Q_LORA = 384
KV_LORA = 256
NOPE_DIM = 128
ROPE_DIM = 64
V_DIM = 128
QK_DIM = NOPE_DIM + ROPE_DIM
HEAD_PAD = 256
ROPE_THETA = 10000.0
LOG2_E = 1.4426950408889634
N_EXPERTS = 32
TOP_K = 4
SWIGLU_LIMIT = 7.0
SWIGLU_ALPHA = 1.702

V7X_VMEM_LIMIT = 56 * 1024 * 1024

POOL_TS = 512
ROUTER_TM = 512
DISPATCH_TM = 512
EXPERT_TM = 512
EXPERT_FC = 512
COMBINE_TM = 256
MLA_TM = 512
ATTN_TQ = 1024
OPROJ_TM = 512
TRIG_ROWS = 512


def _params(*sem):
    return pltpu.CompilerParams(dimension_semantics=sem, vmem_limit_bytes=V7X_VMEM_LIMIT)


def _rms(x, g):
    return x * lax.rsqrt(jnp.mean(x * x, axis=-1, keepdims=True) + EPS) * g


def _dot(a, b):
    return jnp.dot(a, b, preferred_element_type=F32)


SLAB = 8
LANES = 128


def _slab_load(ref, n):
    return jnp.concatenate([ref[pl.ds(s, n, stride=SLAB), :] for s in range(SLAB)], axis=-1)


def _slab_store(ref, x):
    n = x.shape[0]
    for s in range(SLAB):
        ref[pl.ds(s, n, stride=SLAB), :] = x[:, s * LANES:(s + 1) * LANES]


def _pool_kernel(x_ref, g_ref, w_ref, sc_ref, o_ref, hbuf):
    s = pl.program_id(1)
    ts = x_ref.shape[1]
    gd = w_ref.shape[1]
    x = x_ref[0]
    h = _rms(x, g_ref[...])

    @pl.when(s == 0)
    def _():
        hbuf[0:POOL_HALO, :] = jnp.zeros((POOL_HALO, x.shape[1]), F32)

    hbuf[POOL_HALO:, :] = h
    pos1 = s * ts + lax.broadcasted_iota(jnp.int32, (ts, 1), 0) + 1
    outs = []
    for g, win in enumerate(POOL_WINDOWS):
        lo, hi = g * gd, (g + 1) * gd
        hg = h[:, lo:hi]
        acc = hg
        for j in range(1, win):
            acc = acc + hbuf[POOL_HALO - j:POOL_HALO - j + ts, lo:hi]
        cnt = jnp.minimum(pos1, win).astype(F32)
        diff = acc / cnt - hg
        outs.append(_dot(diff.astype(BF16), w_ref[g]))
    mix = jnp.concatenate(outs, axis=-1) * sc_ref[...]
    o_ref[0] = x + mix
    hbuf[0:POOL_HALO, :] = hbuf[ts:ts + POOL_HALO, :]


def _pool_layer(x, g, w, scale):
    B, S, D = x.shape
    ts = POOL_TS
    G, gd, _ = w.shape
    return pl.pallas_call(
        _pool_kernel,
        out_shape=jax.ShapeDtypeStruct((B, S, D), F32),
        grid=(B, S // ts),
        in_specs=[
            pl.BlockSpec((1, ts, D), lambda b, s: (b, s, 0)),
            pl.BlockSpec((1, D), lambda b, s: (0, 0)),
            pl.BlockSpec((G, gd, gd), lambda b, s: (0, 0, 0)),
            pl.BlockSpec((1, D), lambda b, s: (0, 0)),
        ],
        out_specs=pl.BlockSpec((1, ts, D), lambda b, s: (b, s, 0)),
        scratch_shapes=[pltpu.VMEM((ts + POOL_HALO, D), F32)],
        compiler_params=_params("arbitrary", "arbitrary"),
        name="pool_mixer",
    )(x, g.reshape(1, D), w.astype(BF16), scale.reshape(1, D))


def _router_kernel(x_ref, g_ref, rwt_ref, rb_ref, xn_ref, ti_ref, gt_ref, rk_ref, cnt_ref, run):
    i = pl.program_id(0)
    tm = x_ref.shape[0]
    E = rwt_ref.shape[0]

    @pl.when(i == 0)
    def _():
        run[...] = jnp.zeros(run.shape, F32)

    xn = _rms(x_ref[...], g_ref[...])
    _slab_store(xn_ref, xn)
    logits =lax.dot_general(rwt_ref[...], xn, (((1,), (1,)), ((), ())),
                             precision=lax.Precision.HIGHEST,
                             preferred_element_type=F32) + rb_ref[...]
    eidx = lax.broadcasted_iota(jnp.int32, (E, tm), 0)
    vals, sels = [], []
    l = logits
    for k in range(TOP_K):
        m = jnp.max(l, axis=0, keepdims=True)
        idx = jnp.min(jnp.where(l == m, eidx, E), axis=0, keepdims=True)
        sel = eidx == idx
        ti_ref[k:k + 1, :] = idx
        vals.append(m)
        sels.append(sel)
        l = jnp.where(sel, -jnp.inf, l)
    ex = [jnp.exp(v - vals[0]) for v in vals]
    den = ex[0] + ex[1] + ex[2] + ex[3]
    for k in range(TOP_K):
        gt_ref[k:k + 1, :] = ex[k] / den
    member = jnp.zeros((E, tm), F32)
    for sel in sels:
        member = member + sel.astype(F32)
    r = lax.broadcasted_iota(jnp.int32, (tm, tm), 0)
    c = lax.broadcasted_iota(jnp.int32, (tm, tm), 1)
    before = jnp.where(r < c, 1.0, 0.0).astype(BF16)
    prior = _dot(member.astype(BF16), before) + run[:, 0:1]
    for k in range(TOP_K):
        rk = jnp.sum(jnp.where(sels[k], prior, 0.0), axis=0, keepdims=True)
        rk_ref[k:k + 1, :] = rk.astype(jnp.int32)
    run[...] = run[...] + jnp.sum(member, axis=1, keepdims=True)
    cnt_ref[...] = run[...]


def _router(x, g, rw, rb):
    T, D = x.shape
    E = rw.shape[1]
    tm = ROUTER_TM
    row = lambda i: (i, 0)
    col = lambda i: (0, i)
    const = lambda i: (0, 0)
    return pl.pallas_call(
        _router_kernel,
        out_shape=(
            jax.ShapeDtypeStruct((T * SLAB, LANES), F32),
            jax.ShapeDtypeStruct((TOP_K, T), jnp.int32),
            jax.ShapeDtypeStruct((TOP_K, T), F32),
            jax.ShapeDtypeStruct((TOP_K, T), jnp.int32),
            jax.ShapeDtypeStruct((E, 128), F32),
        ),
        grid=(T // tm,),
        in_specs=[
            pl.BlockSpec((tm, D), row),
            pl.BlockSpec((1, D), const),
            pl.BlockSpec((E, D), const),
            pl.BlockSpec((E, 1), const),
        ],
        out_specs=(
            pl.BlockSpec((tm * SLAB, LANES), row),
            pl.BlockSpec((TOP_K, tm), col),
            pl.BlockSpec((TOP_K, tm), col),
            pl.BlockSpec((TOP_K, tm), col),
            pl.BlockSpec((E, 128), const),
        ),
        scratch_shapes=[pltpu.VMEM((E, 128), F32)],
        compiler_params=_params("arbitrary"),
        name="moe_router",
    )(x, g.reshape(1, D), rw.T, rb.reshape(E, 1))


def _routing_tables(top_i, rank, counts, tm, n_tiles):
    counts = counts.astype(jnp.int32)
    padded = ((counts + tm - 1) // tm) * tm
    ends = jnp.cumsum(padded)
    offs = ends - padded
    e_ids = jnp.arange(N_EXPERTS, dtype=jnp.int32)[:, None, None]
    pos = rank + jnp.sum(jnp.where(top_i[None] == e_ids, offs[:, None, None], 0), axis=0)
    n_used = ends[-1] // tm
    tiles = jnp.arange(n_tiles, dtype=jnp.int32)
    te = jnp.sum((ends[None, :] <= (tiles * tm)[:, None]).astype(jnp.int32), axis=1)
    te = jnp.minimum(te, N_EXPERTS - 1)
    te = jnp.where(tiles < n_used, te, jnp.take(te, n_used - 1))
    first = jnp.logical_and(tiles < n_used, te != jnp.concatenate([jnp.full((1,), -1, jnp.int32), te[:-1]]))
    slot = (jnp.cumsum(first.astype(jnp.int32)) - 1) % 2
    nxt_tile = jnp.take(ends, te) // tm
    nxt = jnp.where(nxt_tile < n_used, jnp.take(te, jnp.minimum(nxt_tile, n_tiles - 1)), -1)
    sched = jnp.stack([te, first.astype(jnp.int32), slot, nxt]).astype(jnp.int32)
    cand = jnp.concatenate([ends // tm - 1, tiles])
    valid = jnp.concatenate([padded > 0, tiles >= n_used])
    order = jnp.argsort(jnp.logical_not(valid), stable=True)
    ztiles = jnp.concatenate([jnp.take(cand, order), jnp.sum(valid.astype(jnp.int32)).reshape(1)])
    return pos, sched, n_used.reshape(1).astype(jnp.int32), ztiles.astype(jnp.int32)


def _row_copy(src_ref, src_row, dst_ref, dst_row, sem):
    src = src_ref.at[pl.ds(pl.multiple_of(src_row * SLAB, SLAB), SLAB)]
    dst = dst_ref.at[pl.ds(pl.multiple_of(dst_row * SLAB, SLAB), SLAB)]
    return pltpu.make_async_copy(src, dst, sem)


def _dispatch_kernel(zt_ref, pos_ref, xn_ref, xs_ref, zbuf, sem):
    tm = pos_ref.shape[1]
    zt = zbuf.shape[0]

    @pl.when(pl.program_id(0) == 0)
    def _():
        zbuf[...] = jnp.zeros(zbuf.shape, F32)
        nz = zt_ref[zt_ref.shape[0] - 1]

        def zero_copy(t):
            row = pl.multiple_of(zt_ref[t] * zt, zt)
            return pltpu.make_async_copy(zbuf, xs_ref.at[pl.ds(row, zt)], sem)

        def zstart(t, carry):
            zero_copy(t).start()
            return carry

        def zwait(t, carry):
            zero_copy(t).wait()
            return carry

        lax.fori_loop(0, nz, zstart, 0)
        lax.fori_loop(0, nz, zwait, 0)

    def issue(j, carry):
        for k in range(TOP_K):
            _row_copy(xn_ref, j, xs_ref, pos_ref[k, j], sem).start(priority=k % 2)
        return carry

    lax.fori_loop(0, tm, issue, 0, unroll=8)

    def drain(j, carry):
        for k in range(TOP_K):
            _row_copy(xn_ref, j, xs_ref, pos_ref[k, j], sem).wait()
        return carry

    lax.fori_loop(0, tm, drain, 0, unroll=8)


def _dispatch(xn, pos, ztiles, n_rows):
    T = xn.shape[0] // SLAB
    tm = DISPATCH_TM
    return pl.pallas_call(
        _dispatch_kernel,
        out_shape=jax.ShapeDtypeStruct((n_rows * SLAB, LANES), F32),
        grid_spec=pltpu.PrefetchScalarGridSpec(
            num_scalar_prefetch=1,
            grid=(T // tm,),
            in_specs=[
                pl.BlockSpec((TOP_K, tm), lambda i, zt: (0, i), memory_space=pltpu.SMEM),
                pl.BlockSpec((tm * SLAB, LANES), lambda i, zt: (i, 0)),
            ],
            out_specs=pl.BlockSpec(memory_space=pl.ANY),
            scratch_shapes=[pltpu.VMEM((EXPERT_TM * SLAB, LANES), F32), pltpu.SemaphoreType.DMA],
        ),
        compiler_params=_params("arbitrary"),
        name="moe_dispatch",
    )(ztiles, pos, xn)


SCHED_EXPERT, SCHED_FIRST, SCHED_SLOT, SCHED_NEXT = range(4)


def _expert_kernel(layer, sched_ref, nu_ref, xs_ref, wgu_hbm, bgu_ref, wd_hbm, bd_ref, ys_ref,
                   wgu_f32, wd_f32, wgu_bf, wd_bf, act, sem):
    i = pl.program_id(0)
    F = wd_bf.shape[0]
    e = sched_ref[SCHED_EXPERT, i]
    slot = sched_ref[SCHED_SLOT, i]
    nxt = sched_ref[SCHED_NEXT, i]
    active = i < nu_ref[0]

    def weight_copies(expert, s):
        return (pltpu.make_async_copy(wgu_hbm.at[layer, expert], wgu_f32.at[s], sem.at[0, s]),
                pltpu.make_async_copy(wd_hbm.at[layer, expert], wd_f32.at[s], sem.at[1, s]))

    @pl.when(jnp.logical_and(active, sched_ref[SCHED_FIRST, i] == 1))
    def _():
        @pl.when(i == 0)
        def _():
            for cp in weight_copies(e, slot):
                cp.start()

        @pl.when(nxt >= 0)
        def _():
            for cp in weight_copies(nxt, 1 - slot):
                cp.start()

        for cp in weight_copies(e, slot):
            cp.wait()
        wgu_bf[...] = wgu_f32[slot].astype(BF16)
        wd_bf[...] = wd_f32[slot].astype(BF16)

    @pl.when(active)
    def _():
        x = _slab_load(xs_ref, act.shape[0]).astype(BF16)
        fc = EXPERT_FC
        for c in range(F // fc):
            gate = _dot(x, wgu_bf[:, c * fc:(c + 1) * fc]) + bgu_ref[0, :, c * fc:(c + 1) * fc]
            up = _dot(x, wgu_bf[:, F + c * fc:F + (c + 1) * fc]) + bgu_ref[0, :, F + c * fc:F + (c + 1) * fc]
            gate = jnp.minimum(gate, SWIGLU_LIMIT)
            up = jnp.clip(up, -SWIGLU_LIMIT, SWIGLU_LIMIT)
            a = (up + 1.0) * gate * jax.nn.sigmoid(SWIGLU_ALPHA * gate)
            act[:, c * fc:(c + 1) * fc] = a.astype(BF16)
        _slab_store(ys_ref, _dot(act[...], wd_bf[...]) + bd_ref[0])

    @pl.when(jnp.logical_not(active))
    def _():
        ys_ref[...] = jnp.zeros(ys_ref.shape, F32)


def _experts(xs, sched, n_used, layer, w_gu, b_gu, w_down, b_down):
    P = xs.shape[0] // SLAB
    L, E, D, F2 = w_gu.shape
    F = F2 // 2
    tm = EXPERT_TM
    n_tiles = P // tm
    tile = lambda i, sc, nu: (jnp.minimum(i, nu[0] - 1), 0)
    bsel = lambda i, sc, nu: (layer, sc[SCHED_EXPERT, i], 0, 0)
    return pl.pallas_call(
        functools.partial(_expert_kernel, layer),
        out_shape=jax.ShapeDtypeStruct((P * SLAB, LANES), F32),
        grid_spec=pltpu.PrefetchScalarGridSpec(
            num_scalar_prefetch=2,
            grid=(n_tiles,),
            in_specs=[
                pl.BlockSpec((tm * SLAB, LANES), tile),
                pl.BlockSpec(memory_space=pl.ANY),
                pl.BlockSpec((None, 1, 1, F2), bsel),
                pl.BlockSpec(memory_space=pl.ANY),
                pl.BlockSpec((None, 1, 1, D), bsel),
            ],
            out_specs=pl.BlockSpec((tm * SLAB, LANES), lambda i, sc, nu: (i, 0)),
            scratch_shapes=[
                pltpu.VMEM((2, D, F2), F32),
                pltpu.VMEM((2, F, D), F32),
                pltpu.VMEM((D, F2), BF16),
                pltpu.VMEM((F, D), BF16),
                pltpu.VMEM((tm, F), BF16),
                pltpu.SemaphoreType.DMA((2, 2)),
            ],
        ),
        compiler_params=_params("arbitrary"),
        name="moe_experts",
    )(sched, n_used, xs, w_gu, b_gu.reshape(L, E, 1, F2), w_down, b_down.reshape(L, E, 1, D))


def _combine_kernel(pos_ref, posn_ref, gt_ref, x_ref, ys_ref, p_ref, g_ref, gw_ref, pw_ref, o_ref,
                    buf_a, buf_b, sem):
    i = pl.program_id(0)
    last = pl.num_programs(0) - 1
    tm = x_ref.shape[0]

    def gather_loop(rows_ref, dst, s, wait):
        def body(j, carry):
            for k in range(TOP_K):
                cp = _row_copy(ys_ref, rows_ref[k, j], dst.at[k], j, sem.at[s])
                cp.wait() if wait else cp.start(priority=k % 2)
            return carry

        lax.fori_loop(0, tm, body, 0, unroll=8)

    @pl.when(i == 0)
    def _():
        gather_loop(pos_ref, buf_a, 0, False)

    def step(cur, cur_s, nxt, nxt_s):
        gather_loop(pos_ref, cur, cur_s, True)
        for j in range(tm):
            for k in range(TOP_K):
                _row_copy(ys_ref, posn_ref[k, j], nxt.at[k], j, sem.at[nxt_s]).start(priority=k % 2)
        proj = _dot(p_ref[...].astype(BF16), pw_ref[...])
        gt = gt_ref[...]
        y = x_ref[...]
        for k in range(TOP_K):
            y = y + _slab_load(cur.at[k], tm) * gt[:, k:k + 1]
        h = _rms(y, g_ref[...])
        gate = jax.nn.sigmoid(_dot(h.astype(BF16), gw_ref[...]))
        o_ref[...] = y + gate * proj

        @pl.when(i == last)
        def _():
            gather_loop(posn_ref, nxt, nxt_s, True)

    @pl.when(i % 2 == 0)
    def _():
        step(buf_a, 0, buf_b, 1)

    @pl.when(i % 2 == 1)
    def _():
        step(buf_b, 1, buf_a, 0)


def _combine_ple(x, ys, pos, gates_t, layer, p, g, gate_w, proj_w):
    T, D = x.shape
    PD = p.shape[1]
    tm = COMBINE_TM
    n = T // tm
    row = lambda i: (i, 0)
    const = lambda i: (0, 0)
    return pl.pallas_call(
        _combine_kernel,
        out_shape=jax.ShapeDtypeStruct((T, D), F32),
        grid=(n,),
        in_specs=[
            pl.BlockSpec((TOP_K, tm), lambda i: (0, i), memory_space=pltpu.SMEM),
            pl.BlockSpec((TOP_K, tm), lambda i: (0, jnp.minimum(i + 1, n - 1)), memory_space=pltpu.SMEM),
            pl.BlockSpec((tm, TOP_K), row),
            pl.BlockSpec((tm, D), row),
            pl.BlockSpec(memory_space=pl.ANY),
            pl.BlockSpec((tm, PD), lambda i: (layer * n + i, 0)),
            pl.BlockSpec((1, D), const),
            pl.BlockSpec((D, D), const),
            pl.BlockSpec((PD, D), const),
        ],
        out_specs=pl.BlockSpec((tm, D), row),
        scratch_shapes=[pltpu.VMEM((TOP_K, tm * SLAB, LANES), F32),
                        pltpu.VMEM((TOP_K, tm * SLAB, LANES), F32),
                        pltpu.SemaphoreType.DMA((2,))],
        compiler_params=_params("arbitrary"),
        name="moe_combine_ple",
    )(pos, pos, gates_t.T, x, ys, p, g.reshape(1, D), gate_w.astype(BF16), proj_w.astype(BF16))


def _moe_ple(x, layer, p, norm_ffn, router_w, router_b, w_gu, b_gu, w_down, b_down, ple_norm, gate_w,
             proj_w):
    T, D = x.shape
    n_rows = T * TOP_K + N_EXPERTS * EXPERT_TM
    xn, top_i, gates_t, rank, cnt = _router(x, norm_ffn, router_w, router_b)
    pos, sched, n_used, ztiles = _routing_tables(top_i, rank, cnt[:, 0], EXPERT_TM, n_rows // EXPERT_TM)
    xs = _dispatch(xn, pos, ztiles, n_rows)
    ys = _experts(xs, sched, n_used, layer, w_gu, b_gu, w_down, b_down)
    return _combine_ple(x, ys, pos, gates_t, layer, p, ple_norm, gate_w, proj_w)


def _trig_kernel(pos_ref, f_ref, cos_ref, sin_ref):
    ang = pos_ref[...] * f_ref[...]
    cos_ref[...] = jnp.cos(ang)
    sin_ref[...] = jnp.sin(ang)


def _rope_tables(positions):
    T = positions.size
    half = ROPE_DIM // 2
    per_row = 128 // half
    inv_freq = ROPE_THETA ** (-jnp.arange(half, dtype=F32) / half)
    pos_rep = jnp.repeat(positions.reshape(T // per_row, per_row).astype(F32), half, axis=1)
    rows = T // per_row
    tr = TRIG_ROWS
    spec = pl.BlockSpec((tr, 128), lambda i: (i, 0))
    cos, sin = pl.pallas_call(
        _trig_kernel,
        out_shape=(jax.ShapeDtypeStruct((rows, 128), F32),) * 2,
        grid=(rows // tr,),
        in_specs=[spec, pl.BlockSpec((1, 128), lambda i: (0, 0))],
        out_specs=(spec, spec),
        compiler_params=_params("arbitrary"),
        name="rope_trig",
    )(pos_rep, jnp.tile(inv_freq, per_row).reshape(1, 128))
    cos = cos.reshape(T, half)
    sin = sin.reshape(T, half)
    z = jnp.zeros_like(cos)
    c_tab = jnp.concatenate([cos, cos, z, z], axis=1)
    s_lo = jnp.concatenate([-sin, z, z, z], axis=1)
    s_hi = jnp.concatenate([z, sin, z, z], axis=1)
    return c_tab, s_lo, s_hi


def _rope(blk, c_tab, s_lo, s_hi):
    half = ROPE_DIM // 2
    return (blk * c_tab + pltpu.roll(blk, 128 - half, 1) * s_lo + pltpu.roll(blk, half, 1) * s_hi)


def _mla_pre_kernel(x_ref, g_ref, win_ref, qn_ref, wq_ref, kvn_ref, wkv_ref, gq_ref, gk_ref,
                    c_ref, sl_ref, sh_ref, q_ref, k_ref, v_ref):
    H = q_ref.shape[1]
    h = _rms(x_ref[...], g_ref[...])
    lat = _dot(h.astype(BF16), win_ref[...])
    ql = _rms(lat[:, :Q_LORA], qn_ref[...])
    kvl = _rms(lat[:, Q_LORA:Q_LORA + KV_LORA], kvn_ref[...])
    kr = lat[:, Q_LORA + KV_LORA:]
    q = _dot(ql.astype(BF16), wq_ref[...])
    kv = _dot(kvl.astype(BF16), wkv_ref[...])
    c_tab, s_lo, s_hi = c_ref[...], sl_ref[...], sh_ref[...]
    gq, gk = gq_ref[...], gk_ref[...]
    kr_sq = kr * kr
    kr_rot = _rope(kr * gk[:, NOPE_DIM:], c_tab, s_lo, s_hi)
    qscale = QK_DIM ** -0.5 * LOG2_E
    ones = jnp.ones((NOPE_DIM, NOPE_DIM), BF16)
    for hd in range(H):
        q0 = q[:, hd * HEAD_PAD:hd * HEAD_PAD + NOPE_DIM]
        q1 = q[:, hd * HEAD_PAD + NOPE_DIM:(hd + 1) * HEAD_PAD]
        ss = _dot((q0 * q0 + q1 * q1).astype(BF16), ones)
        inv = lax.rsqrt(ss / QK_DIM + EPS) * qscale
        q_ref[0, hd, :, :NOPE_DIM] = (q0 * inv * gq[:, :NOPE_DIM]).astype(BF16)
        q_ref[0, hd, :, NOPE_DIM:] = (_rope(q1 * gq[:, NOPE_DIM:], c_tab, s_lo, s_hi) * inv).astype(BF16)
        k0 = kv[:, hd * 2 * NOPE_DIM:hd * 2 * NOPE_DIM + NOPE_DIM]
        ssk = _dot((k0 * k0 + kr_sq).astype(BF16), ones)
        invk = lax.rsqrt(ssk / QK_DIM + EPS)
        k_ref[0, hd, :, :NOPE_DIM] = (k0 * invk * gk[:, :NOPE_DIM]).astype(BF16)
        k_ref[0, hd, :, NOPE_DIM:] = (kr_rot * invk).astype(BF16)
        v_ref[0, hd] = kv[:, hd * 2 * NOPE_DIM + NOPE_DIM:(hd + 1) * 2 * NOPE_DIM].T.astype(BF16)


def _mla_pre(x, positions, g, w_in, q_norm, w_qb, kv_norm, w_kvb, qh_norm, kh_norm):
    B, S, D = x.shape
    H = MLA_HEADS
    tm = MLA_TM
    n_lat = Q_LORA + KV_LORA
    w_in_p = jnp.concatenate([w_in, jnp.zeros((D, 128 - ROPE_DIM), F32)], axis=1).astype(BF16)
    wq = w_qb.reshape(Q_LORA, H, QK_DIM)
    wq_p = jnp.concatenate([wq, jnp.zeros((Q_LORA, H, HEAD_PAD - QK_DIM), F32)], axis=2)
    wq_p = wq_p.reshape(Q_LORA, H * HEAD_PAD).astype(BF16)
    pad_gain = lambda n: jnp.concatenate([n, jnp.zeros((HEAD_PAD - QK_DIM,), F32)]).reshape(1, HEAD_PAD)
    c_tab, s_lo, s_hi = _rope_tables(positions)
    row = lambda b, s: (b * (S // tm) + s, 0)
    const = lambda b, s: (0, 0)
    head_spec = lambda w: pl.BlockSpec((1, H, tm, w), lambda b, s: (b, 0, s, 0))
    tab_spec = pl.BlockSpec((tm, 128), row)
    return pl.pallas_call(
        _mla_pre_kernel,
        out_shape=(
            jax.ShapeDtypeStruct((B, H, S, HEAD_PAD), BF16),
            jax.ShapeDtypeStruct((B, H, S, HEAD_PAD), BF16),
            jax.ShapeDtypeStruct((B, H, V_DIM, S), BF16),
        ),
        grid=(B, S // tm),
        in_specs=[
            pl.BlockSpec((tm, D), row),
            pl.BlockSpec((1, D), const),
            pl.BlockSpec((D, n_lat + 128), const),
            pl.BlockSpec((1, Q_LORA), const),
            pl.BlockSpec((Q_LORA, H * HEAD_PAD), const),
            pl.BlockSpec((1, KV_LORA), const),
            pl.BlockSpec((KV_LORA, H * 2 * NOPE_DIM), const),
            pl.BlockSpec((1, HEAD_PAD), const),
            pl.BlockSpec((1, HEAD_PAD), const),
            tab_spec, tab_spec, tab_spec,
        ],
        out_specs=(head_spec(HEAD_PAD), head_spec(HEAD_PAD),
                   pl.BlockSpec((1, H, V_DIM, tm), lambda b, s: (b, 0, 0, s))),
        compiler_params=_params("arbitrary", "arbitrary"),
        name="mla_qkv",
    )(x.reshape(B * S, D), g.reshape(1, D), w_in_p, q_norm.reshape(1, Q_LORA), wq_p,
      kv_norm.reshape(1, KV_LORA), w_kvb.astype(BF16), pad_gain(qh_norm), pad_gain(kh_norm),
      c_tab, s_lo, s_hi)


def _attn_kernel(q_ref, k_ref, vt_ref, o_ref, m_scr, l_scr, acc_scr, st_a, st_b):
    qi = pl.program_id(2)
    tq = q_ref.shape[2]
    q = q_ref[0, 0]
    m_scr[...] = jnp.full(m_scr.shape, -jnp.inf, F32)
    l_scr[...] = jnp.zeros(l_scr.shape, F32)
    acc_scr[...] = jnp.zeros(acc_scr.shape, F32)

    def scores(ki, st_ref):
        start = pl.multiple_of(ki * tq, tq)
        k = k_ref[0, 0, pl.ds(start, tq), :]
        st_ref[...] = lax.dot_general(k, q, (((1,), (1,)), ((), ())), preferred_element_type=F32)

    def fold(ki, st_ref, diagonal):
        start = pl.multiple_of(ki * tq, tq)
        vt = vt_ref[0, 0, :, pl.ds(start, tq)]
        st = st_ref[...]
        if diagonal:
            kc = lax.broadcasted_iota(jnp.int32, (tq, tq), 0) // CHUNK
            qc = lax.broadcasted_iota(jnp.int32, (tq, tq), 1) // CHUNK
            st = jnp.where(kc <= qc, st, -jnp.inf)
        m_prev = m_scr[...]
        m_new = jnp.maximum(m_prev, jnp.max(st, axis=0, keepdims=True))
        alpha = jnp.exp2(m_prev - m_new)
        p = jnp.exp2(st - m_new)
        l_scr[...] = alpha * l_scr[...] + jnp.sum(p, axis=0, keepdims=True)
        acc_scr[...] = alpha * acc_scr[...] + _dot(vt, p.astype(BF16))
        m_scr[...] = m_new

    scores(0, st_a)

    def pair(j, carry):
        scores(2 * j + 1, st_b)
        fold(2 * j, st_a, False)
        scores(2 * j + 2, st_a)
        fold(2 * j + 1, st_b, False)
        return carry

    lax.fori_loop(0, qi // 2, pair, 0)

    @pl.when(qi % 2 == 1)
    def _():
        scores(qi, st_b)
        fold(qi - 1, st_a, False)
        fold(qi, st_b, True)

    @pl.when(qi % 2 == 0)
    def _():
        fold(qi, st_a, True)

    o_ref[0] =(acc_scr[...] / l_scr[...]).T.astype(o_ref.dtype)


def _attention(q, k, vt):
    B, H, S, _ = q.shape
    tq = ATTN_TQ
    return pl.pallas_call(
        _attn_kernel,
        out_shape=jax.ShapeDtypeStruct((B, S, H * V_DIM), BF16),
        grid=(B, H, S // tq),
        in_specs=[
            pl.BlockSpec((1, 1, tq, HEAD_PAD), lambda b, h, i: (b, h, i, 0)),
            pl.BlockSpec((1, 1, S, HEAD_PAD), lambda b, h, i: (b, h, 0, 0)),
            pl.BlockSpec((1, 1, V_DIM, S), lambda b, h, i: (b, h, 0, 0)),
        ],
        out_specs=pl.BlockSpec((1, tq, V_DIM), lambda b, h, i: (b, i, h)),
        scratch_shapes=[
            pltpu.VMEM((1, tq), F32),
            pltpu.VMEM((1, tq), F32),
            pltpu.VMEM((V_DIM, tq), F32),
            pltpu.VMEM((tq, tq), F32),
            pltpu.VMEM((tq, tq), F32),
        ],
        compiler_params=_params("arbitrary", "arbitrary", "arbitrary"),
        name="mla_attention",
    )(q, k, vt)


def _oproj_kernel(x_ref, o_ref, w_ref, out_ref):
    out_ref[...] = x_ref[...] + _dot(o_ref[...], w_ref[...])


def _oproj(x, o, w_o):
    T, D = x.shape
    tm = OPROJ_TM
    row = lambda i: (i, 0)
    return pl.pallas_call(
        _oproj_kernel,
        out_shape=jax.ShapeDtypeStruct((T, D), F32),
        grid=(T // tm,),
        in_specs=[
            pl.BlockSpec((tm, D), row),
            pl.BlockSpec((tm, o.shape[1]), row),
            pl.BlockSpec(w_o.shape, lambda i: (0, 0)),
        ],
        out_specs=pl.BlockSpec((tm, D), row),
        compiler_params=_params("arbitrary"),
        name="mla_oproj",
    )(x, o, w_o.astype(BF16))


def _mla_layer(x, positions, g, w_in, q_norm, w_qb, kv_norm, w_kvb, qh_norm, kh_norm, w_o):
    B, S, D = x.shape
    q, k, v = _mla_pre(x, positions, g, w_in, q_norm, w_qb, kv_norm, w_kvb, qh_norm, kh_norm)
    o = _attention(q, k, v)
    return _oproj(x.reshape(B * S, D), o.reshape(B * S, -1), w_o).reshape(B, S, D)


def kernel(x, p, positions, norm_mix, pool_w, pool_scale, mla_w_in, mla_q_norm, mla_w_qb, mla_kv_norm, mla_w_kvb, mla_qh_norm, mla_kh_norm, mla_w_o, norm_ffn, router_w, router_b, w_gate_up, b_gate_up, w_down, b_down, ple_norm, ple_gate_w, ple_proj_w):
    B, S, D = x.shape
    depth = norm_mix.shape[0]
    p_rows = p.reshape(depth * B * S, -1)
    for i in range(depth):
        j = i // 2
        if i % 2 == 0:
            x = _pool_layer(x, norm_mix[i], pool_w[j], pool_scale[j])
        else:
            x = _mla_layer(x, positions, norm_mix[i], mla_w_in[j], mla_q_norm[j], mla_w_qb[j],
                           mla_kv_norm[j], mla_w_kvb[j], mla_qh_norm[j], mla_kh_norm[j], mla_w_o[j])
        x = _moe_ple(x.reshape(B * S, D), i, p_rows, norm_ffn[i], router_w[i], router_b[i],
                     w_gate_up, b_gate_up, w_down, b_down, ple_norm[i], ple_gate_w[i],
                     ple_proj_w[i]).reshape(B, S, D)
    return x
```

```python
import functools

import jax
import jax.numpy as jnp
from jax import lax
from jax.experimental import pallas as pl
from jax.experimental.pallas import tpu as pltpu

F32 = jnp.float32
BF16 = jnp.bfloat16

EPS = 1e-6
POOL_WINDOWS = (2, 4, 8, 16)
POOL_HALO = 16
CHUNK = 64
MLA_HEADS = 8
Q_LORA = 384
KV_LORA = 256
NOPE_DIM = 128
ROPE_DIM = 64
V_DIM = 128
QK_DIM = NOPE_DIM + ROPE_DIM
HEAD_PAD = 256
ROPE_THETA = 10000.0
LOG2_E = 1.4426950408889634
N_EXPERTS = 32
TOP_K = 4
SWIGLU_LIMIT = 7.0
SWIGLU_ALPHA = 1.702

V7X_VMEM_LIMIT = 56 * 1024 * 1024

POOL_TS = 512
ROUTER_TM = 512
DISPATCH_TM = 512
EXPERT_TM = 512
EXPERT_FC = 512
COMBINE_TM = 256
MLA_TM = 512
ATTN_TQ = 1024
OPROJ_TM = 512
TRIG_ROWS = 512


def _params(*sem):
    return pltpu.CompilerParams(dimension_semantics=sem, vmem_limit_bytes=V7X_VMEM_LIMIT)


def _rms(x, g):
    return x * lax.rsqrt(jnp.mean(x * x, axis=-1, keepdims=True) + EPS) * g


def _dot(a, b):
    return jnp.dot(a, b, preferred_element_type=F32)


SLAB = 8
LANES = 128


def _slab_load(ref, n):
    return jnp.concatenate([ref[pl.ds(s, n, stride=SLAB), :] for s in range(SLAB)], axis=-1)


def _slab_store(ref, x):
    n = x.shape[0]
    for s in range(SLAB):
        ref[pl.ds(s, n, stride=SLAB), :] = x[:, s * LANES:(s + 1) * LANES]


def _pool_kernel(x_ref, g_ref, w_ref, sc_ref, o_ref, hbuf):
    s = pl.program_id(1)
    ts = x_ref.shape[1]
    gd = w_ref.shape[1]
    x = x_ref[0]
    h = _rms(x, g_ref[...])

    @pl.when(s == 0)
    def _():
        hbuf[0:POOL_HALO, :] = jnp.zeros((POOL_HALO, x.shape[1]), F32)

    hbuf[POOL_HALO:, :] = h
    pos1 = s * ts + lax.broadcasted_iota(jnp.int32, (ts, 1), 0) + 1
    outs = []
    for g, win in enumerate(POOL_WINDOWS):
        lo, hi = g * gd, (g + 1) * gd
        hg = h[:, lo:hi]
        acc = hg
        for j in range(1, win):
            acc = acc + hbuf[POOL_HALO - j:POOL_HALO - j + ts, lo:hi]
        cnt = jnp.minimum(pos1, win).astype(F32)
        diff = acc / cnt - hg
        outs.append(_dot(diff.astype(BF16), w_ref[g]))
    mix = jnp.concatenate(outs, axis=-1) * sc_ref[...]
    o_ref[0] = x + mix
    hbuf[0:POOL_HALO, :] = hbuf[ts:ts + POOL_HALO, :]


def _pool_layer(x, g, w, scale):
    B, S, D = x.shape
    ts = POOL_TS
    G, gd, _ = w.shape
    return pl.pallas_call(
        _pool_kernel,
        out_shape=jax.ShapeDtypeStruct((B, S, D), F32),
        grid=(B, S // ts),
        in_specs=[
            pl.BlockSpec((1, ts, D), lambda b, s: (b, s, 0)),
            pl.BlockSpec((1, D), lambda b, s: (0, 0)),
            pl.BlockSpec((G, gd, gd), lambda b, s: (0, 0, 0)),
            pl.BlockSpec((1, D), lambda b, s: (0, 0)),
        ],
        out_specs=pl.BlockSpec((1, ts, D), lambda b, s: (b, s, 0)),
        scratch_shapes=[pltpu.VMEM((ts + POOL_HALO, D), F32)],
        compiler_params=_params("arbitrary", "arbitrary"),
        name="pool_mixer",
    )(x, g.reshape(1, D), w.astype(BF16), scale.reshape(1, D))


def _router_kernel(x_ref, g_ref, rwt_ref, rb_ref, xn_ref, ti_ref, gt_ref, rk_ref, cnt_ref, run):
    i = pl.program_id(0)
    tm = x_ref.shape[0]
    E = rwt_ref.shape[0]

    @pl.when(i == 0)
    def _():
        run[...] = jnp.zeros(run.shape, F32)

    xn = _rms(x_ref[...], g_ref[...])
    _slab_store(xn_ref, xn)
    x_hi = xn.astype(BF16)
    x_lo = (xn - x_hi.astype(F32)).astype(BF16)
    w = rwt_ref[...]
    w_hi = w.astype(BF16)
    w_lo = (w - w_hi.astype(F32)).astype(BF16)
    nt = lambda a, b: lax.dot_general(a, b, (((1,), (1,)), ((), ())), preferred_element_type=F32)
    logits = nt(w_hi, x_hi) + (nt(w_lo, x_hi) + nt(w_hi, x_lo)) + rb_ref[...]
    eidx = lax.broadcasted_iota(jnp.int32, (E, tm), 0)
    vals, sels = [], []
    l = logits
    for k in range(TOP_K):
        m = jnp.max(l, axis=0, keepdims=True)
        idx = jnp.min(jnp.where(l == m, eidx, E), axis=0, keepdims=True)
        sel = eidx == idx
        ti_ref[k:k + 1, :] = idx
        vals.append(m)
        sels.append(sel)
        l = jnp.where(sel, -jnp.inf, l)
    ex = [jnp.exp(v - vals[0]) for v in vals]
    den = ex[0] + ex[1] + ex[2] + ex[3]
    for k in range(TOP_K):
        gt_ref[k:k + 1, :] = ex[k] / den
    member = jnp.zeros((E, tm), F32)
    for sel in sels:
        member = member + sel.astype(F32)
    r = lax.broadcasted_iota(jnp.int32, (tm, tm), 0)
    c = lax.broadcasted_iota(jnp.int32, (tm, tm), 1)
    before = jnp.where(r < c, 1.0, 0.0).astype(BF16)
    prior = _dot(member.astype(BF16), before) + run[:, 0:1]
    for k in range(TOP_K):
        rk = jnp.sum(jnp.where(sels[k], prior, 0.0), axis=0, keepdims=True)
        rk_ref[k:k + 1, :] = rk.astype(jnp.int32)
    run[...] = run[...] + jnp.sum(member, axis=1, keepdims=True)
    cnt_ref[...] = run[...]


def _router(x, g, rw, rb):
    T, D = x.shape
    E = rw.shape[1]
    tm = ROUTER_TM
    row = lambda i: (i, 0)
    col = lambda i: (0, i)
    const = lambda i: (0, 0)
    return pl.pallas_call(
        _router_kernel,
        out_shape=(
            jax.ShapeDtypeStruct((T * SLAB, LANES), F32),
            jax.ShapeDtypeStruct((TOP_K, T), jnp.int32),
            jax.ShapeDtypeStruct((TOP_K, T), F32),
            jax.ShapeDtypeStruct((TOP_K, T), jnp.int32),
            jax.ShapeDtypeStruct((E, 128), F32),
        ),
        grid=(T // tm,),
        in_specs=[
            pl.BlockSpec((tm, D), row),
            pl.BlockSpec((1, D), const),
            pl.BlockSpec((E, D), const),
            pl.BlockSpec((E, 1), const),
        ],
        out_specs=(
            pl.BlockSpec((tm * SLAB, LANES), row),
            pl.BlockSpec((TOP_K, tm), col),
            pl.BlockSpec((TOP_K, tm), col),
            pl.BlockSpec((TOP_K, tm), col),
            pl.BlockSpec((E, 128), const),
        ),
        scratch_shapes=[pltpu.VMEM((E, 128), F32)],
        compiler_params=_params("arbitrary"),
        name="moe_router",
    )(x, g.reshape(1, D), rw.T, rb.reshape(E, 1))


def _routing_tables(top_i, rank, counts, tm, n_tiles):
    counts = counts.astype(jnp.int32)
    padded = ((counts + tm - 1) // tm) * tm
    ends = jnp.cumsum(padded)
    offs = ends - padded
    e_ids = jnp.arange(N_EXPERTS, dtype=jnp.int32)[:, None, None]
    pos = rank + jnp.sum(jnp.where(top_i[None] == e_ids, offs[:, None, None], 0), axis=0)
    n_used = ends[-1] // tm
    tiles = jnp.arange(n_tiles, dtype=jnp.int32)
    te = jnp.sum((ends[None, :] <= (tiles * tm)[:, None]).astype(jnp.int32), axis=1)
    te = jnp.minimum(te, N_EXPERTS - 1)
    te = jnp.where(tiles < n_used, te, jnp.take(te, n_used - 1))
    first = jnp.logical_and(tiles < n_used, te != jnp.concatenate([jnp.full((1,), -1, jnp.int32), te[:-1]]))
    slot = (jnp.cumsum(first.astype(jnp.int32)) - 1) % 2
    nxt_tile = jnp.take(ends, te) // tm
    nxt = jnp.where(nxt_tile < n_used, jnp.take(te, jnp.minimum(nxt_tile, n_tiles - 1)), -1)
    sched = jnp.stack([te, first.astype(jnp.int32), slot, nxt]).astype(jnp.int32)
    cand = jnp.concatenate([ends // tm - 1, tiles])
    valid = jnp.concatenate([padded > 0, tiles >= n_used])
    order = jnp.argsort(jnp.logical_not(valid), stable=True)
    ztiles = jnp.concatenate([jnp.take(cand, order), jnp.sum(valid.astype(jnp.int32)).reshape(1)])
    return pos, sched, n_used.reshape(1).astype(jnp.int32), ztiles.astype(jnp.int32)


def _row_copy(src_ref, src_row, dst_ref, dst_row, sem):
    src = src_ref.at[pl.ds(pl.multiple_of(src_row * SLAB, SLAB), SLAB)]
    dst = dst_ref.at[pl.ds(pl.multiple_of(dst_row * SLAB, SLAB), SLAB)]
    return pltpu.make_async_copy(src, dst, sem)


def _dispatch_kernel(zt_ref, pos_ref, xn_ref, xs_ref, zbuf, sem):
    tm = pos_ref.shape[1]
    zt = zbuf.shape[0]

    @pl.when(pl.program_id(0) == 0)
    def _():
        zbuf[...] = jnp.zeros(zbuf.shape, F32)
        nz = zt_ref[zt_ref.shape[0] - 1]

        def zero_copy(t):
            row = pl.multiple_of(zt_ref[t] * zt, zt)
            return pltpu.make_async_copy(zbuf, xs_ref.at[pl.ds(row, zt)], sem)

        def zstart(t, carry):
            zero_copy(t).start()
            return carry

        def zwait(t, carry):
            zero_copy(t).wait()
            return carry

        lax.fori_loop(0, nz, zstart, 0)
        lax.fori_loop(0, nz, zwait, 0)

    def issue(j, carry):
        for k in range(TOP_K):
            _row_copy(xn_ref, j, xs_ref, pos_ref[k, j], sem).start(priority=k % 2)
        return carry

    lax.fori_loop(0, tm, issue, 0, unroll=8)

    def drain(j, carry):
        for k in range(TOP_K):
            _row_copy(xn_ref, j, xs_ref, pos_ref[k, j], sem).wait()
        return carry

    lax.fori_loop(0, tm, drain, 0, unroll=8)


def _dispatch(xn, pos, ztiles, n_rows):
    T = xn.shape[0] // SLAB
    tm = DISPATCH_TM
    return pl.pallas_call(
        _dispatch_kernel,
        out_shape=jax.ShapeDtypeStruct((n_rows * SLAB, LANES), F32),
        grid_spec=pltpu.PrefetchScalarGridSpec(
            num_scalar_prefetch=1,
            grid=(T // tm,),
            in_specs=[
                pl.BlockSpec((TOP_K, tm), lambda i, zt: (0, i), memory_space=pltpu.SMEM),
                pl.BlockSpec((tm * SLAB, LANES), lambda i, zt: (i, 0)),
            ],
            out_specs=pl.BlockSpec(memory_space=pl.ANY),
            scratch_shapes=[pltpu.VMEM((EXPERT_TM * SLAB, LANES), F32), pltpu.SemaphoreType.DMA],
        ),
        compiler_params=_params("arbitrary"),
        name="moe_dispatch",
    )(ztiles, pos, xn)


SCHED_EXPERT, SCHED_FIRST, SCHED_SLOT, SCHED_NEXT = range(4)


def _expert_kernel(layer, sched_ref, nu_ref, xs_ref, wgu_hbm, bgu_ref, wd_hbm, bd_ref, ys_ref,
                   wgu_f32, wd_f32, wgu_bf, wd_bf, act, sem):
    i = pl.program_id(0)
    F = wd_bf.shape[0]
    e = sched_ref[SCHED_EXPERT, i]
    slot = sched_ref[SCHED_SLOT, i]
    nxt = sched_ref[SCHED_NEXT, i]
    active = i < nu_ref[0]

    def weight_copies(expert, s):
        return (pltpu.make_async_copy(wgu_hbm.at[layer, expert], wgu_f32.at[s], sem.at[0, s]),
                pltpu.make_async_copy(wd_hbm.at[layer, expert], wd_f32.at[s], sem.at[1, s]))

    @pl.when(jnp.logical_and(active, sched_ref[SCHED_FIRST, i] == 1))
    def _():
        @pl.when(i == 0)
        def _():
            for cp in weight_copies(e, slot):
                cp.start()

        @pl.when(nxt >= 0)
        def _():
            for cp in weight_copies(nxt, 1 - slot):
                cp.start()

        for cp in weight_copies(e, slot):
            cp.wait()
        wgu_bf[...] = wgu_f32[slot].astype(BF16)
        wd_bf[...] = wd_f32[slot].astype(BF16)

    @pl.when(active)
    def _():
        x = _slab_load(xs_ref, act.shape[0]).astype(BF16)
        fc = EXPERT_FC
        for c in range(F // fc):
            gate = _dot(x, wgu_bf[:, c * fc:(c + 1) * fc]) + bgu_ref[0, :, c * fc:(c + 1) * fc]
            up = _dot(x, wgu_bf[:, F + c * fc:F + (c + 1) * fc]) + bgu_ref[0, :, F + c * fc:F + (c + 1) * fc]
            gate = jnp.minimum(gate, SWIGLU_LIMIT)
            up = jnp.clip(up, -SWIGLU_LIMIT, SWIGLU_LIMIT)
            a = (up + 1.0) * gate * jax.nn.sigmoid(SWIGLU_ALPHA * gate)
            act[:, c * fc:(c + 1) * fc] = a.astype(BF16)
        _slab_store(ys_ref, _dot(act[...], wd_bf[...]) + bd_ref[0])

    @pl.when(jnp.logical_not(active))
    def _():
        ys_ref[...] = jnp.zeros(ys_ref.shape, F32)


def _experts(xs, sched, n_used, layer, w_gu, b_gu, w_down, b_down):
    P = xs.shape[0] // SLAB
    L, E, D, F2 = w_gu.shape
    F = F2 // 2
    tm = EXPERT_TM
    n_tiles = P // tm
    tile = lambda i, sc, nu: (jnp.minimum(i, nu[0] - 1), 0)
    bsel = lambda i, sc, nu: (layer, sc[SCHED_EXPERT, i], 0, 0)
    return pl.pallas_call(
        functools.partial(_expert_kernel, layer),
        out_shape=jax.ShapeDtypeStruct((P * SLAB, LANES), F32),
        grid_spec=pltpu.PrefetchScalarGridSpec(
            num_scalar_prefetch=2,
            grid=(n_tiles,),
            in_specs=[
                pl.BlockSpec((tm * SLAB, LANES), tile),
                pl.BlockSpec(memory_space=pl.ANY),
                pl.BlockSpec((None, 1, 1, F2), bsel),
                pl.BlockSpec(memory_space=pl.ANY),
                pl.BlockSpec((None, 1, 1, D), bsel),
            ],
            out_specs=pl.BlockSpec((tm * SLAB, LANES), lambda i, sc, nu: (i, 0)),
            scratch_shapes=[
                pltpu.VMEM((2, D, F2), F32),
                pltpu.VMEM((2, F, D), F32),
                pltpu.VMEM((D, F2), BF16),
                pltpu.VMEM((F, D), BF16),
                pltpu.VMEM((tm, F), BF16),
                pltpu.SemaphoreType.DMA((2, 2)),
            ],
        ),
        compiler_params=_params("arbitrary"),
        name="moe_experts",
    )(sched, n_used, xs, w_gu, b_gu.reshape(L, E, 1, F2), w_down, b_down.reshape(L, E, 1, D))


def _combine_kernel(pos_ref, posn_ref, gt_ref, x_ref, ys_ref, p_ref, g_ref, gw_ref, pw_ref, o_ref,
                    buf_a, buf_b, sem):
    i = pl.program_id(0)
    last = pl.num_programs(0) - 1
    tm = x_ref.shape[0]

    def gather_loop(rows_ref, dst, s, wait):
        def body(j, carry):
            for k in range(TOP_K):
                cp = _row_copy(ys_ref, rows_ref[k, j], dst.at[k], j, sem.at[s])
                cp.wait() if wait else cp.start(priority=k % 2)
            return carry

        lax.fori_loop(0, tm, body, 0, unroll=8)

    @pl.when(i == 0)
    def _():
        gather_loop(pos_ref, buf_a, 0, False)

    def step(cur, cur_s, nxt, nxt_s):
        gather_loop(pos_ref, cur, cur_s, True)
        for j in range(tm):
            for k in range(TOP_K):
                _row_copy(ys_ref, posn_ref[k, j], nxt.at[k], j, sem.at[nxt_s]).start(priority=k % 2)
        proj = _dot(p_ref[...].astype(BF16), pw_ref[...])
        gt = gt_ref[...]
        y = x_ref[...]
        for k in range(TOP_K):
            y = y + _slab_load(cur.at[k], tm) * gt[:, k:k + 1]
        h = _rms(y, g_ref[...])
        gate = jax.nn.sigmoid(_dot(h.astype(BF16), gw_ref[...]))
        o_ref[...] = y + gate * proj

        @pl.when(i == last)
        def _():
            gather_loop(posn_ref, nxt, nxt_s, True)

    @pl.when(i % 2 == 0)
    def _():
        step(buf_a, 0, buf_b, 1)

    @pl.when(i % 2 == 1)
    def _():
        step(buf_b, 1, buf_a, 0)


def _combine_ple(x, ys, pos, gates_t, layer, p, g, gate_w, proj_w):
    T, D = x.shape
    PD = p.shape[1]
    tm = COMBINE_TM
    n = T // tm
    row = lambda i: (i, 0)
    const = lambda i: (0, 0)
    return pl.pallas_call(
        _combine_kernel,
        out_shape=jax.ShapeDtypeStruct((T, D), F32),
        grid=(n,),
        in_specs=[
            pl.BlockSpec((TOP_K, tm), lambda i: (0, i), memory_space=pltpu.SMEM),
            pl.BlockSpec((TOP_K, tm), lambda i: (0, jnp.minimum(i + 1, n - 1)), memory_space=pltpu.SMEM),
            pl.BlockSpec((tm, TOP_K), row),
            pl.BlockSpec((tm, D), row),
            pl.BlockSpec(memory_space=pl.ANY),
            pl.BlockSpec((tm, PD), lambda i: (layer * n + i, 0)),
            pl.BlockSpec((1, D), const),
            pl.BlockSpec((D, D), const),
            pl.BlockSpec((PD, D), const),
        ],
        out_specs=pl.BlockSpec((tm, D), row),
        scratch_shapes=[pltpu.VMEM((TOP_K, tm * SLAB, LANES), F32),
                        pltpu.VMEM((TOP_K, tm * SLAB, LANES), F32),
                        pltpu.SemaphoreType.DMA((2,))],
        compiler_params=_params("arbitrary"),
        name="moe_combine_ple",
    )(pos, pos, gates_t.T, x, ys, p, g.reshape(1, D), gate_w.astype(BF16), proj_w.astype(BF16))


def _moe_ple(x, layer, p, norm_ffn, router_w, router_b, w_gu, b_gu, w_down, b_down, ple_norm, gate_w,
             proj_w):
    T, D = x.shape
    n_rows = T * TOP_K + N_EXPERTS * EXPERT_TM
    xn, top_i, gates_t, rank, cnt = _router(x, norm_ffn, router_w, router_b)
    pos, sched, n_used, ztiles = _routing_tables(top_i, rank, cnt[:, 0], EXPERT_TM, n_rows // EXPERT_TM)
    xs = _dispatch(xn, pos, ztiles, n_rows)
    ys = _experts(xs, sched, n_used, layer, w_gu, b_gu, w_down, b_down)
    return _combine_ple(x, ys, pos, gates_t, layer, p, ple_norm, gate_w, proj_w)


def _trig_kernel(pos_ref, f_ref, cos_ref, sin_ref):
    ang = pos_ref[...] * f_ref[...]
    cos_ref[...] = jnp.cos(ang)
    sin_ref[...] = jnp.sin(ang)


def _rope_tables(positions):
    T = positions.size
    half = ROPE_DIM // 2
    per_row = 128 // half
    inv_freq = ROPE_THETA ** (-jnp.arange(half, dtype=F32) / half)
    pos_rep = jnp.repeat(positions.reshape(T // per_row, per_row).astype(F32), half, axis=1)
    rows = T // per_row
    tr = TRIG_ROWS
    spec = pl.BlockSpec((tr, 128), lambda i: (i, 0))
    cos, sin = pl.pallas_call(
        _trig_kernel,
        out_shape=(jax.ShapeDtypeStruct((rows, 128), F32),) * 2,
        grid=(rows // tr,),
        in_specs=[spec, pl.BlockSpec((1, 128), lambda i: (0, 0))],
        out_specs=(spec, spec),
        compiler_params=_params("arbitrary"),
        name="rope_trig",
    )(pos_rep, jnp.tile(inv_freq, per_row).reshape(1, 128))
    cos = cos.reshape(T, half)
    sin = sin.reshape(T, half)
    z = jnp.zeros_like(cos)
    c_tab = jnp.concatenate([cos, cos, z, z], axis=1)
    s_lo = jnp.concatenate([-sin, z, z, z], axis=1)
    s_hi = jnp.concatenate([z, sin, z, z], axis=1)
    return c_tab, s_lo, s_hi


def _rope(blk, c_tab, s_lo, s_hi):
    half = ROPE_DIM // 2
    return (blk * c_tab + pltpu.roll(blk, 128 - half, 1) * s_lo + pltpu.roll(blk, half, 1) * s_hi)


def _mla_pre_kernel(x_ref, g_ref, win_ref, qn_ref, wq_ref, kvn_ref, wkv_ref, gq_ref, gk_ref,
                    c_ref, sl_ref, sh_ref, q_ref, k_ref, v_ref):
    H = q_ref.shape[1]
    h = _rms(x_ref[...], g_ref[...])
    lat = _dot(h.astype(BF16), win_ref[...])
    ql = _rms(lat[:, :Q_LORA], qn_ref[...])
    kvl = _rms(lat[:, Q_LORA:Q_LORA + KV_LORA], kvn_ref[...])
    kr = lat[:, Q_LORA + KV_LORA:]
    q = _dot(ql.astype(BF16), wq_ref[...])
    kv = _dot(kvl.astype(BF16), wkv_ref[...])
    c_tab, s_lo, s_hi = c_ref[...], sl_ref[...], sh_ref[...]
    gq, gk = gq_ref[...], gk_ref[...]
    kr_sq = kr * kr
    kr_rot = _rope(kr * gk[:, NOPE_DIM:], c_tab, s_lo, s_hi)
    qscale = QK_DIM ** -0.5 * LOG2_E
    ones = jnp.ones((NOPE_DIM, NOPE_DIM), BF16)
    for hd in range(H):
        q0 = q[:, hd * HEAD_PAD:hd * HEAD_PAD + NOPE_DIM]
        q1 = q[:, hd * HEAD_PAD + NOPE_DIM:(hd + 1) * HEAD_PAD]
        ss = _dot((q0 * q0 + q1 * q1).astype(BF16), ones)
        inv = lax.rsqrt(ss / QK_DIM + EPS) * qscale
        q_ref[0, hd, :, :NOPE_DIM] = (q0 * inv * gq[:, :NOPE_DIM]).astype(BF16)
        q_ref[0, hd, :, NOPE_DIM:] = (_rope(q1 * gq[:, NOPE_DIM:], c_tab, s_lo, s_hi) * inv).astype(BF16)
        k0 = kv[:, hd * 2 * NOPE_DIM:hd * 2 * NOPE_DIM + NOPE_DIM]
        ssk = _dot((k0 * k0 + kr_sq).astype(BF16), ones)
        invk = lax.rsqrt(ssk / QK_DIM + EPS)
        k_ref[0, hd, :, :NOPE_DIM] = (k0 * invk * gk[:, :NOPE_DIM]).astype(BF16)
        k_ref[0, hd, :, NOPE_DIM:] = (kr_rot * invk).astype(BF16)
        v_ref[0, hd] = kv[:, hd * 2 * NOPE_DIM + NOPE_DIM:(hd + 1) * 2 * NOPE_DIM].T.astype(BF16)


def _mla_pre(x, positions, g, w_in, q_norm, w_qb, kv_norm, w_kvb, qh_norm, kh_norm):
    B, S, D = x.shape
    H = MLA_HEADS
    tm = MLA_TM
    n_lat = Q_LORA + KV_LORA
    w_in_p = jnp.concatenate([w_in, jnp.zeros((D, 128 - ROPE_DIM), F32)], axis=1).astype(BF16)
    wq = w_qb.reshape(Q_LORA, H, QK_DIM)
    wq_p = jnp.concatenate([wq, jnp.zeros((Q_LORA, H, HEAD_PAD - QK_DIM), F32)], axis=2)
    wq_p = wq_p.reshape(Q_LORA, H * HEAD_PAD).astype(BF16)
    pad_gain = lambda n: jnp.concatenate([n, jnp.zeros((HEAD_PAD - QK_DIM,), F32)]).reshape(1, HEAD_PAD)
    c_tab, s_lo, s_hi = _rope_tables(positions)
    row = lambda b, s: (b * (S // tm) + s, 0)
    const = lambda b, s: (0, 0)
    head_spec = lambda w: pl.BlockSpec((1, H, tm, w), lambda b, s: (b, 0, s, 0))
    tab_spec = pl.BlockSpec((tm, 128), row)
    return pl.pallas_call(
        _mla_pre_kernel,
        out_shape=(
            jax.ShapeDtypeStruct((B, H, S, HEAD_PAD), BF16),
            jax.ShapeDtypeStruct((B, H, S, HEAD_PAD), BF16),
            jax.ShapeDtypeStruct((B, H, V_DIM, S), BF16),
        ),
        grid=(B, S // tm),
        in_specs=[
            pl.BlockSpec((tm, D), row),
            pl.BlockSpec((1, D), const),
            pl.BlockSpec((D, n_lat + 128), const),
            pl.BlockSpec((1, Q_LORA), const),
            pl.BlockSpec((Q_LORA, H * HEAD_PAD), const),
            pl.BlockSpec((1, KV_LORA), const),
            pl.BlockSpec((KV_LORA, H * 2 * NOPE_DIM), const),
            pl.BlockSpec((1, HEAD_PAD), const),
            pl.BlockSpec((1, HEAD_PAD), const),
            tab_spec, tab_spec, tab_spec,
        ],
        out_specs=(head_spec(HEAD_PAD), head_spec(HEAD_PAD),
                   pl.BlockSpec((1, H, V_DIM, tm), lambda b, s: (b, 0, 0, s))),
        compiler_params=_params("arbitrary", "arbitrary"),
        name="mla_qkv",
    )(x.reshape(B * S, D), g.reshape(1, D), w_in_p, q_norm.reshape(1, Q_LORA), wq_p,
      kv_norm.reshape(1, KV_LORA), w_kvb.astype(BF16), pad_gain(qh_norm), pad_gain(kh_norm),
      c_tab, s_lo, s_hi)


def _attn_kernel(q_ref, k_ref, vt_ref, o_ref, m_scr, l_scr, acc_scr, st_a, st_b):
    qi = pl.program_id(2)
    tq = q_ref.shape[2]
    q = q_ref[0, 0]
    m_scr[...] = jnp.full(m_scr.shape, -jnp.inf, F32)
    l_scr[...] = jnp.zeros(l_scr.shape, F32)
    acc_scr[...] = jnp.zeros(acc_scr.shape, F32)

    def scores(ki, st_ref):
        start = pl.multiple_of(ki * tq, tq)
        k = k_ref[0, 0, pl.ds(start, tq), :]
        st_ref[...] = lax.dot_general(k, q, (((1,), (1,)), ((), ())), preferred_element_type=F32)

    def fold(ki, st_ref, diagonal):
        start = pl.multiple_of(ki * tq, tq)
        vt = vt_ref[0, 0, :, pl.ds(start, tq)]
        st = st_ref[...]
        if diagonal:
            kc = lax.broadcasted_iota(jnp.int32, (tq, tq), 0) // CHUNK
            qc = lax.broadcasted_iota(jnp.int32, (tq, tq), 1) // CHUNK
            st = jnp.where(kc <= qc, st, -jnp.inf)
        m_prev = m_scr[...]
        m_new = jnp.maximum(m_prev, jnp.max(st, axis=0, keepdims=True))
        alpha = jnp.exp2(m_prev - m_new)
        p = jnp.exp2(st - m_new)
        l_scr[...] = alpha * l_scr[...] + jnp.sum(p, axis=0, keepdims=True)
        acc_scr[...] = alpha * acc_scr[...] + _dot(vt, p.astype(BF16))
        m_scr[...] = m_new

    scores(0, st_a)

    def pair(j, carry):
        scores(2 * j + 1, st_b)
        fold(2 * j, st_a, False)
        scores(2 * j + 2, st_a)
        fold(2 * j + 1, st_b, False)
        return carry

    lax.fori_loop(0, qi // 2, pair, 0)

    @pl.when(qi % 2 == 1)
    def _():
        scores(qi, st_b)
        fold(qi - 1, st_a, False)
        fold(qi, st_b, True)

    @pl.when(qi % 2 == 0)
    def _():
        fold(qi, st_a, True)

    o_ref[0] =(acc_scr[...] / l_scr[...]).T.astype(o_ref.dtype)


def _attention(q, k, vt):
    B, H, S, _ = q.shape
    tq = ATTN_TQ
    return pl.pallas_call(
        _attn_kernel,
        out_shape=jax.ShapeDtypeStruct((B, S, H * V_DIM), BF16),
        grid=(B, H, S // tq),
        in_specs=[
            pl.BlockSpec((1, 1, tq, HEAD_PAD), lambda b, h, i: (b, h, i, 0)),
            pl.BlockSpec((1, 1, S, HEAD_PAD), lambda b, h, i: (b, h, 0, 0)),
            pl.BlockSpec((1, 1, V_DIM, S), lambda b, h, i: (b, h, 0, 0)),
        ],
        out_specs=pl.BlockSpec((1, tq, V_DIM), lambda b, h, i: (b, i, h)),
        scratch_shapes=[
            pltpu.VMEM((1, tq), F32),
            pltpu.VMEM((1, tq), F32),
            pltpu.VMEM((V_DIM, tq), F32),
            pltpu.VMEM((tq, tq), F32),
            pltpu.VMEM((tq, tq), F32),
        ],
        compiler_params=_params("arbitrary", "arbitrary", "arbitrary"),
        name="mla_attention",
    )(q, k, vt)


def _oproj_kernel(x_ref, o_ref, w_ref, out_ref):
    out_ref[...] = x_ref[...] + _dot(o_ref[...], w_ref[...])


def _oproj(x, o, w_o):
    T, D = x.shape
    tm = OPROJ_TM
    row = lambda i: (i, 0)
    return pl.pallas_call(
        _oproj_kernel,
        out_shape=jax.ShapeDtypeStruct((T, D), F32),
        grid=(T // tm,),
        in_specs=[
            pl.BlockSpec((tm, D), row),
            pl.BlockSpec((tm, o.shape[1]), row),
            pl.BlockSpec(w_o.shape, lambda i: (0, 0)),
        ],
        out_specs=pl.BlockSpec((tm, D), row),
        compiler_params=_params("arbitrary"),
        name="mla_oproj",
    )(x, o, w_o.astype(BF16))


def _mla_layer(x, positions, g, w_in, q_norm, w_qb, kv_norm, w_kvb, qh_norm, kh_norm, w_o):
    B, S, D = x.shape
    q, k, v = _mla_pre(x, positions, g, w_in, q_norm, w_qb, kv_norm, w_kvb, qh_norm, kh_norm)
    o = _attention(q, k, v)
    return _oproj(x.reshape(B * S, D), o.reshape(B * S, -1), w_o).reshape(B, S, D)


def kernel(x, p, positions, norm_mix, pool_w, pool_scale, mla_w_in, mla_q_norm, mla_w_qb, mla_kv_norm, mla_w_kvb, mla_qh_norm, mla_kh_norm, mla_w_o, norm_ffn, router_w, router_b, w_gate_up, b_gate_up, w_down, b_down, ple_norm, ple_gate_w, ple_proj_w):
    B, S, D = x.shape
    depth = norm_mix.shape[0]
    p_rows = p.reshape(depth * B * S, -1)
    for i in range(depth):
        j = i // 2
        if i % 2 == 0:
            x = _pool_layer(x, norm_mix[i], pool_w[j], pool_scale[j])
        else:
            x = _mla_layer(x, positions, norm_mix[i], mla_w_in[j], mla_q_norm[j], mla_w_qb[j],
                           mla_kv_norm[j], mla_w_kvb[j], mla_qh_norm[j], mla_kh_norm[j], mla_w_o[j])
        x = _moe_ple(x.reshape(B * S, D), i, p_rows, norm_ffn[i], router_w[i], router_b[i],
                     w_gate_up, b_gate_up, w_down, b_down, ple_norm[i], ple_gate_w[i],
                     ple_proj_w[i]).reshape(B, S, D)
    return x
```

```python
import functools

import jax
import jax.numpy as jnp
from jax import lax
from jax.experimental import pallas as pl
from jax.experimental.pallas import tpu as pltpu

F32 = jnp.float32
BF16 = jnp.bfloat16

EPS = 1e-6
POOL_WINDOWS = (2, 4, 8, 16)
POOL_HALO = 16
CHUNK = 64
MLA_HEADS = 8
Q_LORA = 384
KV_LORA = 256
NOPE_DIM = 128
ROPE_DIM = 64
V_DIM = 128
QK_DIM = NOPE_DIM + ROPE_DIM
HEAD_PAD = 256
ROPE_THETA = 10000.0
LOG2_E = 1.4426950408889634
N_EXPERTS = 32
TOP_K = 4
SWIGLU_LIMIT = 7.0
SWIGLU_ALPHA = 1.702

V7X_VMEM_LIMIT = 56 * 1024 * 1024

POOL_TS = 512
ROUTER_TM = 512
DISPATCH_TM = 512
EXPERT_TM = 512
EXPERT_FC = 512
COMBINE_TM = 256
MLA_TM = 512
ATTN_TQ = 1024
OPROJ_TM = 512
TRIG_ROWS = 512


def _params(*sem):
    return pltpu.CompilerParams(dimension_semantics=sem, vmem_limit_bytes=V7X_VMEM_LIMIT)


def _rms(x, g):
    return x * lax.rsqrt(jnp.mean(x * x, axis=-1, keepdims=True) + EPS) * g


def _dot(a, b):
    return jnp.dot(a, b, preferred_element_type=F32)


SLAB = 8
LANES = 128


def _slab_load(ref, n):
    return jnp.concatenate([ref[pl.ds(s, n, stride=SLAB), :] for s in range(SLAB)], axis=-1)


def _slab_store(ref, x):
    n = x.shape[0]
    for s in range(SLAB):
        ref[pl.ds(s, n, stride=SLAB), :] = x[:, s * LANES:(s + 1) * LANES]


def _pool_kernel(x_ref, g_ref, w_ref, sc_ref, o_ref, hbuf):
    s = pl.program_id(1)
    ts = x_ref.shape[1]
    gd = w_ref.shape[1]
    x = x_ref[0]
    h = _rms(x, g_ref[...])

    @pl.when(s == 0)
    def _():
        hbuf[0:POOL_HALO, :] = jnp.zeros((POOL_HALO, x.shape[1]), F32)

    hbuf[POOL_HALO:, :] = h
    pos1 = s * ts + lax.broadcasted_iota(jnp.int32, (ts, 1), 0) + 1
    outs = []
    for g, win in enumerate(POOL_WINDOWS):
        lo, hi = g * gd, (g + 1) * gd
        hg = h[:, lo:hi]
        acc = hg
        for j in range(1, win):
            acc = acc + hbuf[POOL_HALO - j:POOL_HALO - j + ts, lo:hi]
        cnt = jnp.minimum(pos1, win).astype(F32)
        diff = acc / cnt - hg
        outs.append(_dot(diff.astype(BF16), w_ref[g]))
    mix = jnp.concatenate(outs, axis=-1) * sc_ref[...]
    o_ref[0] = x + mix
    hbuf[0:POOL_HALO, :] = hbuf[ts:ts + POOL_HALO, :]


def _pool_layer(x, g, w, scale):
    B, S, D = x.shape
    ts = POOL_TS
    G, gd, _ = w.shape
    return pl.pallas_call(
        _pool_kernel,
        out_shape=jax.ShapeDtypeStruct((B, S, D), F32),
        grid=(B, S // ts),
        in_specs=[
            pl.BlockSpec((1, ts, D), lambda b, s: (b, s, 0)),
            pl.BlockSpec((1, D), lambda b, s: (0, 0)),
            pl.BlockSpec((G, gd, gd), lambda b, s: (0, 0, 0)),
            pl.BlockSpec((1, D), lambda b, s: (0, 0)),
        ],
        out_specs=pl.BlockSpec((1, ts, D), lambda b, s: (b, s, 0)),
        scratch_shapes=[pltpu.VMEM((ts + POOL_HALO, D), F32)],
        compiler_params=_params("arbitrary", "arbitrary"),
        name="pool_mixer",
    )(x, g.reshape(1, D), w.astype(BF16), scale.reshape(1, D))


def _router_kernel(x_ref, g_ref, rwt_ref, rb_ref, xn_ref, ti_ref, gt_ref, rk_ref, cnt_ref, run):
    i = pl.program_id(0)
    tm = x_ref.shape[0]
    E = rwt_ref.shape[0]

    @pl.when(i == 0)
    def _():
        run[...] = jnp.zeros(run.shape, F32)

    xn = _rms(x_ref[...], g_ref[...])
    _slab_store(xn_ref, xn)
    x_hi = xn.astype(BF16)
    x_lo = (xn - x_hi.astype(F32)).astype(BF16)
    w = rwt_ref[...]
    w_hi = w.astype(BF16)
    w_lo = (w - w_hi.astype(F32)).astype(BF16)
    nt = lambda a, b: lax.dot_general(a, b, (((1,), (1,)), ((), ())), preferred_element_type=F32)
    logits = nt(w_hi, x_hi) + (nt(w_lo, x_hi) + nt(w_hi, x_lo)) + rb_ref[...]
    eidx = lax.broadcasted_iota(jnp.int32, (E, tm), 0)
    vals, sels = [], []
    l = logits
    for k in range(TOP_K):
        m = jnp.max(l, axis=0, keepdims=True)
        idx = jnp.min(jnp.where(l == m, eidx, E), axis=0, keepdims=True)
        sel = eidx == idx
        ti_ref[k:k + 1, :] = idx
        vals.append(m)
        sels.append(sel)
        l = jnp.where(sel, -jnp.inf, l)
    ex = [jnp.exp(v - vals[0]) for v in vals]
    den = ex[0] + ex[1] + ex[2] + ex[3]
    for k in range(TOP_K):
        gt_ref[k:k + 1, :] = ex[k] / den
    member = jnp.zeros((E, tm), F32)
    for sel in sels:
        member = member + sel.astype(F32)
    r = lax.broadcasted_iota(jnp.int32, (tm, tm), 0)
    c = lax.broadcasted_iota(jnp.int32, (tm, tm), 1)
    before = jnp.where(r < c, 1.0, 0.0).astype(BF16)
    prior = _dot(member.astype(BF16), before) + run[:, 0:1]
    for k in range(TOP_K):
        rk = jnp.sum(jnp.where(sels[k], prior, 0.0), axis=0, keepdims=True)
        rk_ref[k:k + 1, :] = rk.astype(jnp.int32)
    run[...] = run[...] + jnp.sum(member, axis=1, keepdims=True)
    cnt_ref[...] = run[...]


def _router(x, g, rw, rb):
    T, D = x.shape
    E = rw.shape[1]
    tm = ROUTER_TM
    row = lambda i: (i, 0)
    col = lambda i: (0, i)
    const = lambda i: (0, 0)
    return pl.pallas_call(
        _router_kernel,
        out_shape=(
            jax.ShapeDtypeStruct((T * SLAB, LANES), F32),
            jax.ShapeDtypeStruct((TOP_K, T), jnp.int32),
            jax.ShapeDtypeStruct((TOP_K, T), F32),
            jax.ShapeDtypeStruct((TOP_K, T), jnp.int32),
            jax.ShapeDtypeStruct((E, 128), F32),
        ),
        grid=(T // tm,),
        in_specs=[
            pl.BlockSpec((tm, D), row),
            pl.BlockSpec((1, D), const),
            pl.BlockSpec((E, D), const),
            pl.BlockSpec((E, 1), const),
        ],
        out_specs=(
            pl.BlockSpec((tm * SLAB, LANES), row),
            pl.BlockSpec((TOP_K, tm), col),
            pl.BlockSpec((TOP_K, tm), col),
            pl.BlockSpec((TOP_K, tm), col),
            pl.BlockSpec((E, 128), const),
        ),
        scratch_shapes=[pltpu.VMEM((E, 128), F32)],
        compiler_params=_params("arbitrary"),
        name="moe_router",
    )(x, g.reshape(1, D), rw.T, rb.reshape(E, 1))


def _routing_tables(top_i, rank, counts, tm, n_tiles):
    counts = counts.astype(jnp.int32)
    padded = ((counts + tm - 1) // tm) * tm
    ends = jnp.cumsum(padded)
    offs = ends - padded
    e_ids = jnp.arange(N_EXPERTS, dtype=jnp.int32)[:, None, None]
    pos = rank + jnp.sum(jnp.where(top_i[None] == e_ids, offs[:, None, None], 0), axis=0)
    n_used = ends[-1] // tm
    tiles = jnp.arange(n_tiles, dtype=jnp.int32)
    te = jnp.sum((ends[None, :] <= (tiles * tm)[:, None]).astype(jnp.int32), axis=1)
    te = jnp.minimum(te, N_EXPERTS - 1)
    te = jnp.where(tiles < n_used, te, jnp.take(te, n_used - 1))
    first = jnp.logical_and(tiles < n_used, te != jnp.concatenate([jnp.full((1,), -1, jnp.int32), te[:-1]]))
    slot = (jnp.cumsum(first.astype(jnp.int32)) - 1) % 2
    nxt_tile = jnp.take(ends, te) // tm
    nxt = jnp.where(nxt_tile < n_used, jnp.take(te, jnp.minimum(nxt_tile, n_tiles - 1)), -1)
    rows = jnp.clip(jnp.take(counts, te) - (tiles * tm - jnp.take(offs, te)), 0, tm)
    sched = jnp.stack([te, first.astype(jnp.int32), slot, nxt, rows]).astype(jnp.int32)
    cand = jnp.concatenate([ends // tm - 1, tiles])
    valid = jnp.concatenate([padded > 0, tiles >= n_used])
    order = jnp.argsort(jnp.logical_not(valid), stable=True)
    ztiles = jnp.concatenate([jnp.take(cand, order), jnp.sum(valid.astype(jnp.int32)).reshape(1)])
    return pos, sched, n_used.reshape(1).astype(jnp.int32), ztiles.astype(jnp.int32)


def _row_copy(src_ref, src_row, dst_ref, dst_row, sem):
    src = src_ref.at[pl.ds(pl.multiple_of(src_row * SLAB, SLAB), SLAB)]
    dst = dst_ref.at[pl.ds(pl.multiple_of(dst_row * SLAB, SLAB), SLAB)]
    return pltpu.make_async_copy(src, dst, sem)


def _dispatch_kernel(zt_ref, pos_ref, xn_ref, xs_ref, zbuf, sem):
    tm = pos_ref.shape[1]
    zt = zbuf.shape[0]

    @pl.when(pl.program_id(0) == 0)
    def _():
        zbuf[...] = jnp.zeros(zbuf.shape, F32)
        nz = zt_ref[zt_ref.shape[0] - 1]

        def zero_copy(t):
            row = pl.multiple_of(zt_ref[t] * zt, zt)
            return pltpu.make_async_copy(zbuf, xs_ref.at[pl.ds(row, zt)], sem)

        def zstart(t, carry):
            zero_copy(t).start()
            return carry

        def zwait(t, carry):
            zero_copy(t).wait()
            return carry

        lax.fori_loop(0, nz, zstart, 0)
        lax.fori_loop(0, nz, zwait, 0)

    def issue(j, carry):
        for k in range(TOP_K):
            _row_copy(xn_ref, j, xs_ref, pos_ref[k, j], sem).start(priority=k % 2)
        return carry

    lax.fori_loop(0, tm, issue, 0, unroll=8)

    def drain(j, carry):
        for k in range(TOP_K):
            _row_copy(xn_ref, j, xs_ref, pos_ref[k, j], sem).wait()
        return carry

    lax.fori_loop(0, tm, drain, 0, unroll=8)


def _dispatch(xn, pos, ztiles, n_rows):
    T = xn.shape[0] // SLAB
    tm = DISPATCH_TM
    return pl.pallas_call(
        _dispatch_kernel,
        out_shape=jax.ShapeDtypeStruct((n_rows * SLAB, LANES), F32),
        grid_spec=pltpu.PrefetchScalarGridSpec(
            num_scalar_prefetch=1,
            grid=(T // tm,),
            in_specs=[
                pl.BlockSpec((TOP_K, tm), lambda i, zt: (0, i), memory_space=pltpu.SMEM),
                pl.BlockSpec((tm * SLAB, LANES), lambda i, zt: (i, 0)),
            ],
            out_specs=pl.BlockSpec(memory_space=pl.ANY),
            scratch_shapes=[pltpu.VMEM((EXPERT_TM * SLAB, LANES), F32), pltpu.SemaphoreType.DMA],
        ),
        compiler_params=_params("arbitrary"),
        name="moe_dispatch",
    )(ztiles, pos, xn)


SCHED_EXPERT, SCHED_FIRST, SCHED_SLOT, SCHED_NEXT, SCHED_ROWS = range(5)


def _expert_kernel(layer, sched_ref, nu_ref, xs_ref, wgu_hbm, bgu_ref, wd_hbm, bd_ref, ys_ref,
                   wgu_f32, wd_f32, wgu_bf, wd_bf, act, sem):
    i = pl.program_id(0)
    F = wd_bf.shape[0]
    e = sched_ref[SCHED_EXPERT, i]
    slot = sched_ref[SCHED_SLOT, i]
    nxt = sched_ref[SCHED_NEXT, i]
    active = i < nu_ref[0]

    def weight_copies(expert, s):
        return (pltpu.make_async_copy(wgu_hbm.at[layer, expert], wgu_f32.at[s], sem.at[0, s]),
                pltpu.make_async_copy(wd_hbm.at[layer, expert], wd_f32.at[s], sem.at[1, s]))

    @pl.when(jnp.logical_and(active, sched_ref[SCHED_FIRST, i] == 1))
    def _():
        @pl.when(i == 0)
        def _():
            for cp in weight_copies(e, slot):
                cp.start()

        @pl.when(nxt >= 0)
        def _():
            for cp in weight_copies(nxt, 1 - slot):
                cp.start()

        for cp in weight_copies(e, slot):
            cp.wait()
        wgu_bf[...] = wgu_f32[slot].astype(BF16)
        wd_bf[...] = wd_f32[slot].astype(BF16)

    tm = act.shape[0]
    rows = sched_ref[SCHED_ROWS, i]

    def compute(n):
        x = _slab_load(xs_ref, n).astype(BF16)
        fc = EXPERT_FC
        for c in range(F // fc):
            gate = _dot(x, wgu_bf[:, c * fc:(c + 1) * fc]) + bgu_ref[0, :, c * fc:(c + 1) * fc]
            up = _dot(x, wgu_bf[:, F + c * fc:F + (c + 1) * fc]) + bgu_ref[0, :, F + c * fc:F + (c + 1) * fc]
            gate = jnp.minimum(gate, SWIGLU_LIMIT)
            up = jnp.clip(up, -SWIGLU_LIMIT, SWIGLU_LIMIT)
            a = (up + 1.0) * gate * jax.nn.sigmoid(SWIGLU_ALPHA * gate)
            act[0:n, c * fc:(c + 1) * fc] = a.astype(BF16)
        _slab_store(ys_ref, _dot(act[0:n, :], wd_bf[...]) + bd_ref[0])
        if n < tm:
            ys_ref[n * SLAB:, :] = jnp.zeros(((tm - n) * SLAB, LANES), F32)

    @pl.when(jnp.logical_and(active, rows > tm // 2))
    def _():
        compute(tm)

    @pl.when(jnp.logical_and(active, rows <= tm // 2))
    def _():
        compute(tm // 2)

    @pl.when(jnp.logical_not(active))
    def _():
        ys_ref[...] = jnp.zeros(ys_ref.shape, F32)


def _experts(xs, sched, n_used, layer, w_gu, b_gu, w_down, b_down):
    P = xs.shape[0] // SLAB
    L, E, D, F2 = w_gu.shape
    F = F2 // 2
    tm = EXPERT_TM
    n_tiles = P // tm
    tile = lambda i, sc, nu: (jnp.minimum(i, nu[0] - 1), 0)
    bsel = lambda i, sc, nu: (layer, sc[SCHED_EXPERT, i], 0, 0)
    return pl.pallas_call(
        functools.partial(_expert_kernel, layer),
        out_shape=jax.ShapeDtypeStruct((P * SLAB, LANES), F32),
        grid_spec=pltpu.PrefetchScalarGridSpec(
            num_scalar_prefetch=2,
            grid=(n_tiles,),
            in_specs=[
                pl.BlockSpec((tm * SLAB, LANES), tile),
                pl.BlockSpec(memory_space=pl.ANY),
                pl.BlockSpec((None, 1, 1, F2), bsel),
                pl.BlockSpec(memory_space=pl.ANY),
                pl.BlockSpec((None, 1, 1, D), bsel),
            ],
            out_specs=pl.BlockSpec((tm * SLAB, LANES), lambda i, sc, nu: (i, 0)),
            scratch_shapes=[
                pltpu.VMEM((2, D, F2), F32),
                pltpu.VMEM((2, F, D), F32),
                pltpu.VMEM((D, F2), BF16),
                pltpu.VMEM((F, D), BF16),
                pltpu.VMEM((tm, F), BF16),
                pltpu.SemaphoreType.DMA((2, 2)),
            ],
        ),
        compiler_params=_params("arbitrary"),
        name="moe_experts",
    )(sched, n_used, xs, w_gu, b_gu.reshape(L, E, 1, F2), w_down, b_down.reshape(L, E, 1, D))


def _combine_kernel(pos_ref, posn_ref, gt_ref, x_ref, ys_ref, p_ref, g_ref, gw_ref, pw_ref, o_ref,
                    buf_a, buf_b, sem):
    i = pl.program_id(0)
    last = pl.num_programs(0) - 1
    tm = x_ref.shape[0]

    def gather_loop(rows_ref, dst, s, wait):
        def body(j, carry):
            for k in range(TOP_K):
                cp = _row_copy(ys_ref, rows_ref[k, j], dst.at[k], j, sem.at[s])
                cp.wait() if wait else cp.start(priority=k % 2)
            return carry

        lax.fori_loop(0, tm, body, 0, unroll=8)

    @pl.when(i == 0)
    def _():
        gather_loop(pos_ref, buf_a, 0, False)

    def step(cur, cur_s, nxt, nxt_s):
        gather_loop(pos_ref, cur, cur_s, True)
        for j in range(tm):
            for k in range(TOP_K):
                _row_copy(ys_ref, posn_ref[k, j], nxt.at[k], j, sem.at[nxt_s]).start(priority=k % 2)
        proj = _dot(p_ref[...].astype(BF16), pw_ref[...])
        gt = gt_ref[...]
        y = x_ref[...]
        for k in range(TOP_K):
            y = y + _slab_load(cur.at[k], tm) * gt[:, k:k + 1]
        h = _rms(y, g_ref[...])
        gate = jax.nn.sigmoid(_dot(h.astype(BF16), gw_ref[...]))
        o_ref[...] = y + gate * proj

        @pl.when(i == last)
        def _():
            gather_loop(posn_ref, nxt, nxt_s, True)

    @pl.when(i % 2 == 0)
    def _():
        step(buf_a, 0, buf_b, 1)

    @pl.when(i % 2 == 1)
    def _():
        step(buf_b, 1, buf_a, 0)


def _combine_ple(x, ys, pos, gates_t, layer, p, g, gate_w, proj_w):
    T, D = x.shape
    PD = p.shape[1]
    tm = COMBINE_TM
    n = T // tm
    row = lambda i: (i, 0)
    const = lambda i: (0, 0)
    return pl.pallas_call(
        _combine_kernel,
        out_shape=jax.ShapeDtypeStruct((T, D), F32),
        grid=(n,),
        in_specs=[
            pl.BlockSpec((TOP_K, tm), lambda i: (0, i), memory_space=pltpu.SMEM),
            pl.BlockSpec((TOP_K, tm), lambda i: (0, jnp.minimum(i + 1, n - 1)), memory_space=pltpu.SMEM),
            pl.BlockSpec((tm, TOP_K), row),
            pl.BlockSpec((tm, D), row),
            pl.BlockSpec(memory_space=pl.ANY),
            pl.BlockSpec((tm, PD), lambda i: (layer * n + i, 0)),
            pl.BlockSpec((1, D), const),
            pl.BlockSpec((D, D), const),
            pl.BlockSpec((PD, D), const),
        ],
        out_specs=pl.BlockSpec((tm, D), row),
        scratch_shapes=[pltpu.VMEM((TOP_K, tm * SLAB, LANES), F32),
                        pltpu.VMEM((TOP_K, tm * SLAB, LANES), F32),
                        pltpu.SemaphoreType.DMA((2,))],
        compiler_params=_params("arbitrary"),
        name="moe_combine_ple",
    )(pos, pos, gates_t.T, x, ys, p, g.reshape(1, D), gate_w.astype(BF16), proj_w.astype(BF16))


def _moe_ple(x, layer, p, norm_ffn, router_w, router_b, w_gu, b_gu, w_down, b_down, ple_norm, gate_w,
             proj_w):
    T, D = x.shape
    n_rows = T * TOP_K + N_EXPERTS * EXPERT_TM
    xn, top_i, gates_t, rank, cnt = _router(x, norm_ffn, router_w, router_b)
    pos, sched, n_used, ztiles = _routing_tables(top_i, rank, cnt[:, 0], EXPERT_TM, n_rows // EXPERT_TM)
    xs = _dispatch(xn, pos, ztiles, n_rows)
    ys = _experts(xs, sched, n_used, layer, w_gu, b_gu, w_down, b_down)
    return _combine_ple(x, ys, pos, gates_t, layer, p, ple_norm, gate_w, proj_w)


def _trig_kernel(pos_ref, f_ref, cos_ref, sin_ref):
    ang = pos_ref[...] * f_ref[...]
    cos_ref[...] = jnp.cos(ang)
    sin_ref[...] = jnp.sin(ang)


def _rope_tables(positions):
    T = positions.size
    half = ROPE_DIM // 2
    per_row = 128 // half
    inv_freq = ROPE_THETA ** (-jnp.arange(half, dtype=F32) / half)
    pos_rep = jnp.repeat(positions.reshape(T // per_row, per_row).astype(F32), half, axis=1)
    rows = T // per_row
    tr = TRIG_ROWS
    spec = pl.BlockSpec((tr, 128), lambda i: (i, 0))
    cos, sin = pl.pallas_call(
        _trig_kernel,
        out_shape=(jax.ShapeDtypeStruct((rows, 128), F32),) * 2,
        grid=(rows // tr,),
        in_specs=[spec, pl.BlockSpec((1, 128), lambda i: (0, 0))],
        out_specs=(spec, spec),
        compiler_params=_params("arbitrary"),
        name="rope_trig",
    )(pos_rep, jnp.tile(inv_freq, per_row).reshape(1, 128))
    cos = cos.reshape(T, half)
    sin = sin.reshape(T, half)
    z = jnp.zeros_like(cos)
    c_tab = jnp.concatenate([cos, cos, z, z], axis=1)
    s_lo = jnp.concatenate([-sin, z, z, z], axis=1)
    s_hi = jnp.concatenate([z, sin, z, z], axis=1)
    return c_tab, s_lo, s_hi


def _rope(blk, c_tab, s_lo, s_hi):
    half = ROPE_DIM // 2
    return (blk * c_tab + pltpu.roll(blk, 128 - half, 1) * s_lo + pltpu.roll(blk, half, 1) * s_hi)


def _mla_pre_kernel(x_ref, g_ref, win_ref, qn_ref, wq_ref, kvn_ref, wkv_ref, gq_ref, gk_ref,
                    c_ref, sl_ref, sh_ref, q_ref, k_ref, v_ref):
    H = q_ref.shape[1]
    h = _rms(x_ref[...], g_ref[...])
    lat = _dot(h.astype(BF16), win_ref[...])
    ql = _rms(lat[:, :Q_LORA], qn_ref[...])
    kvl = _rms(lat[:, Q_LORA:Q_LORA + KV_LORA], kvn_ref[...])
    kr = lat[:, Q_LORA + KV_LORA:]
    q = _dot(ql.astype(BF16), wq_ref[...])
    kv = _dot(kvl.astype(BF16), wkv_ref[...])
    c_tab, s_lo, s_hi = c_ref[...], sl_ref[...], sh_ref[...]
    gq, gk = gq_ref[...], gk_ref[...]
    kr_sq = kr * kr
    kr_rot = _rope(kr * gk[:, NOPE_DIM:], c_tab, s_lo, s_hi)
    qscale = QK_DIM ** -0.5 * LOG2_E
    ones = jnp.ones((NOPE_DIM, NOPE_DIM), BF16)
    for hd in range(H):
        q0 = q[:, hd * HEAD_PAD:hd * HEAD_PAD + NOPE_DIM]
        q1 = q[:, hd * HEAD_PAD + NOPE_DIM:(hd + 1) * HEAD_PAD]
        ss = _dot((q0 * q0 + q1 * q1).astype(BF16), ones)
        inv = lax.rsqrt(ss / QK_DIM + EPS) * qscale
        q_ref[0, hd, :, :NOPE_DIM] = (q0 * inv * gq[:, :NOPE_DIM]).astype(BF16)
        q_ref[0, hd, :, NOPE_DIM:] = (_rope(q1 * gq[:, NOPE_DIM:], c_tab, s_lo, s_hi) * inv).astype(BF16)
        k0 = kv[:, hd * 2 * NOPE_DIM:hd * 2 * NOPE_DIM + NOPE_DIM]
        ssk = _dot((k0 * k0 + kr_sq).astype(BF16), ones)
        invk = lax.rsqrt(ssk / QK_DIM + EPS)
        k_ref[0, hd, :, :NOPE_DIM] = (k0 * invk * gk[:, :NOPE_DIM]).astype(BF16)
        k_ref[0, hd, :, NOPE_DIM:] = (kr_rot * invk).astype(BF16)
        v_ref[0, hd] = kv[:, hd * 2 * NOPE_DIM + NOPE_DIM:(hd + 1) * 2 * NOPE_DIM].T.astype(BF16)


def _mla_pre(x, positions, g, w_in, q_norm, w_qb, kv_norm, w_kvb, qh_norm, kh_norm):
    B, S, D = x.shape
    H = MLA_HEADS
    tm = MLA_TM
    n_lat = Q_LORA + KV_LORA
    w_in_p = jnp.concatenate([w_in, jnp.zeros((D, 128 - ROPE_DIM), F32)], axis=1).astype(BF16)
    wq = w_qb.reshape(Q_LORA, H, QK_DIM)
    wq_p = jnp.concatenate([wq, jnp.zeros((Q_LORA, H, HEAD_PAD - QK_DIM), F32)], axis=2)
    wq_p = wq_p.reshape(Q_LORA, H * HEAD_PAD).astype(BF16)
    pad_gain = lambda n: jnp.concatenate([n, jnp.zeros((HEAD_PAD - QK_DIM,), F32)]).reshape(1, HEAD_PAD)
    c_tab, s_lo, s_hi = _rope_tables(positions)
    row = lambda b, s: (b * (S // tm) + s, 0)
    const = lambda b, s: (0, 0)
    head_spec = lambda w: pl.BlockSpec((1, H, tm, w), lambda b, s: (b, 0, s, 0))
    tab_spec = pl.BlockSpec((tm, 128), row)
    return pl.pallas_call(
        _mla_pre_kernel,
        out_shape=(
            jax.ShapeDtypeStruct((B, H, S, HEAD_PAD), BF16),
            jax.ShapeDtypeStruct((B, H, S, HEAD_PAD), BF16),
            jax.ShapeDtypeStruct((B, H, V_DIM, S), BF16),
        ),
        grid=(B, S // tm),
        in_specs=[
            pl.BlockSpec((tm, D), row),
            pl.BlockSpec((1, D), const),
            pl.BlockSpec((D, n_lat + 128), const),
            pl.BlockSpec((1, Q_LORA), const),
            pl.BlockSpec((Q_LORA, H * HEAD_PAD), const),
            pl.BlockSpec((1, KV_LORA), const),
            pl.BlockSpec((KV_LORA, H * 2 * NOPE_DIM), const),
            pl.BlockSpec((1, HEAD_PAD), const),
            pl.BlockSpec((1, HEAD_PAD), const),
            tab_spec, tab_spec, tab_spec,
        ],
        out_specs=(head_spec(HEAD_PAD), head_spec(HEAD_PAD),
                   pl.BlockSpec((1, H, V_DIM, tm), lambda b, s: (b, 0, 0, s))),
        compiler_params=_params("arbitrary", "arbitrary"),
        name="mla_qkv",
    )(x.reshape(B * S, D), g.reshape(1, D), w_in_p, q_norm.reshape(1, Q_LORA), wq_p,
      kv_norm.reshape(1, KV_LORA), w_kvb.astype(BF16), pad_gain(qh_norm), pad_gain(kh_norm),
      c_tab, s_lo, s_hi)


def _attn_kernel(q_ref, k_ref, vt_ref, o_ref, m_scr, l_scr, acc_scr, st_a, st_b):
    qi = pl.program_id(2)
    tq = q_ref.shape[2]
    q = q_ref[0, 0]
    m_scr[...] = jnp.full(m_scr.shape, -jnp.inf, F32)
    l_scr[...] = jnp.zeros(l_scr.shape, F32)
    acc_scr[...] = jnp.zeros(acc_scr.shape, F32)

    def scores(ki, st_ref):
        start = pl.multiple_of(ki * tq, tq)
        k = k_ref[0, 0, pl.ds(start, tq), :]
        st_ref[...] = lax.dot_general(k, q, (((1,), (1,)), ((), ())), preferred_element_type=F32)

    def fold(ki, st_ref, diagonal):
        start = pl.multiple_of(ki * tq, tq)
        vt = vt_ref[0, 0, :, pl.ds(start, tq)]
        st = st_ref[...]
        if diagonal:
            kc = lax.broadcasted_iota(jnp.int32, (tq, tq), 0) // CHUNK
            qc = lax.broadcasted_iota(jnp.int32, (tq, tq), 1) // CHUNK
            st = jnp.where(kc <= qc, st, -jnp.inf)
        m_prev = m_scr[...]
        m_new = jnp.maximum(m_prev, jnp.max(st, axis=0, keepdims=True))
        alpha = jnp.exp2(m_prev - m_new)
        p = jnp.exp2(st - m_new)
        l_scr[...] = alpha * l_scr[...] + jnp.sum(p, axis=0, keepdims=True)
        acc_scr[...] = alpha * acc_scr[...] + _dot(vt, p.astype(BF16))
        m_scr[...] = m_new

    scores(0, st_a)

    def pair(j, carry):
        scores(2 * j + 1, st_b)
        fold(2 * j, st_a, False)
        scores(2 * j + 2, st_a)
        fold(2 * j + 1, st_b, False)
        return carry

    lax.fori_loop(0, qi // 2, pair, 0)

    @pl.when(qi % 2 == 1)
    def _():
        scores(qi, st_b)
        fold(qi - 1, st_a, False)
        fold(qi, st_b, True)

    @pl.when(qi % 2 == 0)
    def _():
        fold(qi, st_a, True)

    o_ref[0] =(acc_scr[...] / l_scr[...]).T.astype(o_ref.dtype)


def _attention(q, k, vt):
    B, H, S, _ = q.shape
    tq = ATTN_TQ
    return pl.pallas_call(
        _attn_kernel,
        out_shape=jax.ShapeDtypeStruct((B, S, H * V_DIM), BF16),
        grid=(B, H, S // tq),
        in_specs=[
            pl.BlockSpec((1, 1, tq, HEAD_PAD), lambda b, h, i: (b, h, i, 0)),
            pl.BlockSpec((1, 1, S, HEAD_PAD), lambda b, h, i: (b, h, 0, 0)),
            pl.BlockSpec((1, 1, V_DIM, S), lambda b, h, i: (b, h, 0, 0)),
        ],
        out_specs=pl.BlockSpec((1, tq, V_DIM), lambda b, h, i: (b, i, h)),
        scratch_shapes=[
            pltpu.VMEM((1, tq), F32),
            pltpu.VMEM((1, tq), F32),
            pltpu.VMEM((V_DIM, tq), F32),
            pltpu.VMEM((tq, tq), F32),
            pltpu.VMEM((tq, tq), F32),
        ],
        compiler_params=_params("arbitrary", "arbitrary", "arbitrary"),
        name="mla_attention",
    )(q, k, vt)


def _oproj_kernel(x_ref, o_ref, w_ref, out_ref):
    out_ref[...] = x_ref[...] + _dot(o_ref[...], w_ref[...])


def _oproj(x, o, w_o):
    T, D = x.shape
    tm = OPROJ_TM
    row = lambda i: (i, 0)
    return pl.pallas_call(
        _oproj_kernel,
        out_shape=jax.ShapeDtypeStruct((T, D), F32),
        grid=(T // tm,),
        in_specs=[
            pl.BlockSpec((tm, D), row),
            pl.BlockSpec((tm, o.shape[1]), row),
            pl.BlockSpec(w_o.shape, lambda i: (0, 0)),
        ],
        out_specs=pl.BlockSpec((tm, D), row),
        compiler_params=_params("arbitrary"),
        name="mla_oproj",
    )(x, o, w_o.astype(BF16))


def _mla_layer(x, positions, g, w_in, q_norm, w_qb, kv_norm, w_kvb, qh_norm, kh_norm, w_o):
    B, S, D = x.shape
    q, k, v = _mla_pre(x, positions, g, w_in, q_norm, w_qb, kv_norm, w_kvb, qh_norm, kh_norm)
    o = _attention(q, k, v)
    return _oproj(x.reshape(B * S, D), o.reshape(B * S, -1), w_o).reshape(B, S, D)


def kernel(x, p, positions, norm_mix, pool_w, pool_scale, mla_w_in, mla_q_norm, mla_w_qb, mla_kv_norm, mla_w_kvb, mla_qh_norm, mla_kh_norm, mla_w_o, norm_ffn, router_w, router_b, w_gate_up, b_gate_up, w_down, b_down, ple_norm, ple_gate_w, ple_proj_w):
    B, S, D = x.shape
    depth = norm_mix.shape[0]
    p_rows = p.reshape(depth * B * S, -1)
    for i in range(depth):
        j = i // 2
        if i % 2 == 0:
            x = _pool_layer(x, norm_mix[i], pool_w[j], pool_scale[j])
        else:
            x = _mla_layer(x, positions, norm_mix[i], mla_w_in[j], mla_q_norm[j], mla_w_qb[j],
                           mla_kv_norm[j], mla_w_kvb[j], mla_qh_norm[j], mla_kh_norm[j], mla_w_o[j])
        x = _moe_ple(x.reshape(B * S, D), i, p_rows, norm_ffn[i], router_w[i], router_b[i],
                     w_gate_up, b_gate_up, w_down, b_down, ple_norm[i], ple_gate_w[i],
                     ple_proj_w[i]).reshape(B, S, D)
    return x
```
